```python
import jax, jax.numpy as jnp
from jax import lax
import numpy as np

D_MODEL = 1024
BATCH = 1
SEQ = 16384
DEPTH = 4
DEC_BATCH = 2
DEC_SEQ = 16384
PAST_LEN = 128

MLA_HEADS = 8
NOPE_DIM = 64
ROPE_DIM = 32
QK_DIM = NOPE_DIM + ROPE_DIM
MLA_V_DIM = 64
Q_LORA = 256
KV_LORA = 128
MLA_WIDTH = MLA_HEADS * MLA_V_DIM
ROPE_THETA = 10000.0
Q_BLOCK = 128
GLA_HEADS = 4
GLA_KEY = D_MODEL // 2
GLA_VAL = D_MODEL
GLA_DK = GLA_KEY // GLA_HEADS
GLA_DV = GLA_VAL // GLA_HEADS
GATE_RANK = 16
GATE_NORM = 16.0
CHUNK = 64
D_FF = 2816
EPS = 1e-6
P_IN = Q_LORA + KV_LORA + ROPE_DIM + 2 * GLA_KEY + GLA_VAL + 2 * GATE_RANK + GLA_VAL + 2 * D_MODEL

kernel_name = "hybrid_mla_gla_macaron_encoder"


def _rmsnorm(x, g):
    xf = x.astype(jnp.float32)
    y = xf * lax.rsqrt(jnp.mean(xf * xf, axis=-1, keepdims=True) + EPS)
    return (y * g.astype(jnp.float32)).astype(x.dtype)


def _swiglu(u, w_in, w_out):
    a, b = jnp.split(u @ w_in, 2, axis=-1)
    return (jax.nn.silu(a) * b) @ w_out


def _rope_tables(s):
    inv_freq = 1.0 / (ROPE_THETA ** (jnp.arange(0, ROPE_DIM, 2, dtype=jnp.float32) / ROPE_DIM))
    ang = jnp.arange(s, dtype=jnp.float32)[:, None] * inv_freq[None, :]
    return jnp.cos(ang)[:, None, :], jnp.sin(ang)[:, None, :]


def _rope(x, cos, sin):
    half = ROPE_DIM // 2
    c, s = cos.astype(x.dtype), sin.astype(x.dtype)
    x1, x2 = x[..., :half], x[..., half:]
    return jnp.concatenate([x1 * c - x2 * s, x2 * c + x1 * s], axis=-1)


def _mla(cq, ckv, kpe, q_norm_g, w_uq, kv_norm_g, w_ukv):
    b, s, _ = cq.shape
    cos, sin = _rope_tables(s)
    q = (_rmsnorm(cq, q_norm_g) @ w_uq).reshape(b, s, MLA_HEADS, QK_DIM)
    q = jnp.concatenate([q[..., :NOPE_DIM], _rope(q[..., NOPE_DIM:], cos, sin)], axis=-1)
    kv = (_rmsnorm(ckv, kv_norm_g) @ w_ukv).reshape(b, s, MLA_HEADS, NOPE_DIM + MLA_V_DIM)
    k_nope, v = kv[..., :NOPE_DIM], kv[..., NOPE_DIM:]
    k_pe = _rope(kpe[:, :, None, :], cos, sin)
    k = jnp.concatenate([k_nope, jnp.broadcast_to(k_pe, (b, s, MLA_HEADS, ROPE_DIM))], axis=-1)
    scale = QK_DIM ** -0.5
    nq = s // Q_BLOCK
    qb = q.reshape(b, nq, Q_BLOCK, MLA_HEADS, QK_DIM).transpose(1, 0, 2, 3, 4)

    def attend(q_blk):
        sc = jnp.einsum('bqhd,bkhd->bhqk', q_blk, k).astype(jnp.float32) * scale
        p = jax.nn.softmax(sc, axis=-1).astype(v.dtype)
        return jnp.einsum('bhqk,bkhd->bqhd', p, v)

    o = lax.map(attend, qb)
    return o.transpose(1, 0, 2, 3, 4).reshape(b, s, MLA_WIDTH)


def _gla_chunked(q, k, v, g):
    b, s, h, dk = q.shape
    dv = v.shape[-1]
    n = s // CHUNK
    to_c = lambda t: t.reshape(b, n, CHUNK, h, t.shape[-1]).transpose(0, 3, 1, 2, 4)
    q, k, v, g = to_c(q), to_c(k), to_c(v), to_c(g)
    cb = jnp.cumsum(g, axis=3)
    cb_last = cb[:, :, :, -1:, :]
    q_t = q * jnp.exp(cb)
    k_t = k * jnp.exp(-cb)
    k_s = k * jnp.exp(cb_last - cb)
    mask = jnp.tril(jnp.ones((CHUNK, CHUNK), dtype=bool))
    att = jnp.where(mask, jnp.einsum('bhncd,bhned->bhnce', q_t, k_t), 0.0)
    o_intra = jnp.einsum('bhnce,bhnev->bhncv', att, v)
    u = jnp.einsum('bhncd,bhncv->bhndv', k_s, v)
    decay = jnp.exp(cb_last[:, :, :, 0, :])

    def step(state, inp):
        d, uc = inp
        return d[..., None] * state + uc, state

    init = jnp.zeros((b, h, dk, dv), jnp.float32)
    _, s_prev = lax.scan(step, init, (jnp.moveaxis(decay, 2, 0), jnp.moveaxis(u, 2, 0)))
    s_prev = jnp.moveaxis(s_prev, 0, 2)
    o = o_intra + jnp.einsum('bhncd,bhndv->bhncv', q_t, s_prev)
    return o.transpose(0, 2, 3, 1, 4).reshape(b, s, h, dv)


def _gla(gq, gk, gv, ga, gog, wa2_f, ba_f, wa2_b, ba_b, norm_g):
    b, s, _ = gq.shape
    f32 = jnp.float32
    q = gq.reshape(b, s, GLA_HEADS, GLA_DK).astype(f32) * (GLA_DK ** -0.5)
    k = gk.reshape(b, s, GLA_HEADS, GLA_DK).astype(f32)
    v = gv.reshape(b, s, GLA_HEADS, GLA_DV).astype(f32)
    ga_f, ga_b = ga[..., :GATE_RANK], ga[..., GATE_RANK:]
    g_f = (jax.nn.log_sigmoid((ga_f @ wa2_f + ba_f).astype(f32)) / GATE_NORM).reshape(b, s, GLA_HEADS, GLA_DK)
    g_b = (jax.nn.log_sigmoid((ga_b @ wa2_b + ba_b).astype(f32)) / GATE_NORM).reshape(b, s, GLA_HEADS, GLA_DK)
    fl = lambda t: jnp.flip(t, axis=1)
    o = _gla_chunked(q, k, v, g_f) + fl(_gla_chunked(fl(q), fl(k), fl(v), fl(g_b)))
    o = _rmsnorm(o, norm_g)
    o = o * jax.nn.silu(gog.reshape(b, s, GLA_HEADS, GLA_DV).astype(f32))
    return o.reshape(b, s, GLA_VAL).astype(gq.dtype)


def _split_in(z):
    sizes = (Q_LORA, KV_LORA, ROPE_DIM, GLA_KEY, GLA_KEY, GLA_VAL, 2 * GATE_RANK, GLA_VAL, 2 * D_MODEL)
    idx = np.cumsum(sizes)[:-1].tolist()
    return jnp.split(z, idx, axis=-1)


def _trunk(x, ffn1_pre_g, ffn1_w_in, ffn1_w_out, ffn1_post_g, mix_pre_g, w_in,
           q_norm_g, w_uq, kv_norm_g, w_ukv, w_oa, gla_wa2_f, gla_ba_f, gla_wa2_b, gla_ba_b,
           gla_norm_g, w_ob, w_out, mix_post_g, ffn2_pre_g, ffn2_w_in, ffn2_w_out, ffn2_post_g):
    for l in range(DEPTH):
        x = x + 0.5 * _rmsnorm(_swiglu(_rmsnorm(x, ffn1_pre_g[l]), ffn1_w_in[l], ffn1_w_out[l]), ffn1_post_g[l])
        u = _rmsnorm(x, mix_pre_g[l])
        cq, ckv, kpe, gq, gk, gv, ga, gog, bg = _split_in(u @ w_in[l])
        o_a = _mla(cq, ckv, kpe, q_norm_g[l], w_uq[l], kv_norm_g[l], w_ukv[l]) @ w_oa[l]
        o_b = _gla(gq, gk, gv, ga, gog, gla_wa2_f[l], gla_ba_f[l], gla_wa2_b[l], gla_ba_b[l], gla_norm_g[l]) @ w_ob[l]
        gate = jax.nn.sigmoid(bg)
        merged = gate[..., :D_MODEL] * o_a + gate[..., D_MODEL:] * o_b
        x = x + _rmsnorm(merged @ w_out[l], mix_post_g[l])
        x = x + 0.5 * _rmsnorm(_swiglu(_rmsnorm(x, ffn2_pre_g[l]), ffn2_w_in[l], ffn2_w_out[l]), ffn2_post_g[l])
    return x


def setup_inputs(seed: int = 0) -> dict:
    key = jax.random.key(seed)
    ks = iter(jax.random.split(key, 32))
    nrm = lambda shape, fan_in: jax.random.normal(next(ks), shape, jnp.float32) * (fan_in ** -0.5)
    gain = lambda n: 1.0 + 0.05 * jax.random.normal(next(ks), (DEPTH, n), jnp.float32)
    bias = lambda n: 0.01 * jax.random.normal(next(ks), (DEPTH, n), jnp.float32)
    return {
        "x_prompt": jax.random.normal(next(ks), (BATCH, SEQ, D_MODEL), jnp.float32),
        "x_sample": jax.random.normal(next(ks), (DEC_BATCH, DEC_SEQ, D_MODEL), jnp.float32),
        "ffn1_pre_g": gain(D_MODEL),
        "ffn1_w_in": nrm((DEPTH, D_MODEL, 2 * D_FF), D_MODEL),
        "ffn1_w_out": nrm((DEPTH, D_FF, D_MODEL), D_FF),
        "ffn1_post_g": gain(D_MODEL),
        "mix_pre_g": gain(D_MODEL),
        "w_in": nrm((DEPTH, D_MODEL, P_IN), D_MODEL),
        "q_norm_g": gain(Q_LORA),
        "w_uq": nrm((DEPTH, Q_LORA, MLA_HEADS * QK_DIM), Q_LORA),
        "kv_norm_g": gain(KV_LORA),
        "w_ukv": nrm((DEPTH, KV_LORA, MLA_HEADS * (NOPE_DIM + MLA_V_DIM)), KV_LORA),
        "w_oa": nrm((DEPTH, MLA_WIDTH, D_MODEL), MLA_WIDTH),
        "gla_wa2_f": nrm((DEPTH, GATE_RANK, GLA_KEY), GATE_RANK),
        "gla_ba_f": bias(GLA_KEY),
        "gla_wa2_b": nrm((DEPTH, GATE_RANK, GLA_KEY), GATE_RANK),
        "gla_ba_b": bias(GLA_KEY),
        "gla_norm_g": gain(GLA_DV),
        "w_ob": nrm((DEPTH, GLA_VAL, D_MODEL), GLA_VAL),
        "w_out": nrm((DEPTH, D_MODEL, D_MODEL), D_MODEL),
        "mix_post_g": gain(D_MODEL),
        "ffn2_pre_g": gain(D_MODEL),
        "ffn2_w_in": nrm((DEPTH, D_MODEL, 2 * D_FF), D_MODEL),
        "ffn2_w_out": nrm((DEPTH, D_FF, D_MODEL), D_FF),
        "ffn2_post_g": gain(D_MODEL),
    }


def reference(x_prompt, x_sample, ffn1_pre_g, ffn1_w_in, ffn1_w_out, ffn1_post_g, mix_pre_g, w_in,
              q_norm_g, w_uq, kv_norm_g, w_ukv, w_oa, gla_wa2_f, gla_ba_f, gla_wa2_b, gla_ba_b,
              gla_norm_g, w_ob, w_out, mix_post_g, ffn2_pre_g, ffn2_w_in, ffn2_w_out, ffn2_post_g):
    y_prompt = _trunk(x_prompt, ffn1_pre_g, ffn1_w_in, ffn1_w_out, ffn1_post_g, mix_pre_g, w_in,
                      q_norm_g, w_uq, kv_norm_g, w_ukv, w_oa, gla_wa2_f, gla_ba_f, gla_wa2_b, gla_ba_b,
                      gla_norm_g, w_ob, w_out, mix_post_g, ffn2_pre_g, ffn2_w_in, ffn2_w_out, ffn2_post_g)
    y_sample = _trunk(x_sample, ffn1_pre_g, ffn1_w_in, ffn1_w_out, ffn1_post_g, mix_pre_g, w_in,
                      q_norm_g, w_uq, kv_norm_g, w_ukv, w_oa, gla_wa2_f, gla_ba_f, gla_wa2_b, gla_ba_b,
                      gla_norm_g, w_ob, w_out, mix_post_g, ffn2_pre_g, ffn2_w_in, ffn2_w_out, ffn2_post_g)
    return (y_prompt, y_sample)
```

```python
import functools

import jax
import jax.numpy as jnp
from jax import lax
from jax.experimental import pallas as pl
from jax.experimental.pallas import tpu as pltpu

F32 = jnp.float32
BF16 = jnp.bfloat16

EPS = 1e-6
MLA_HEADS = 8
NOPE_DIM = 64
ROPE_DIM = 32
MLA_V_DIM = 64
ROPE_THETA = 10000.0
GLA_HEADS = 4
GATE_RANK = 16
GATE_NORM = 16.0
GLA_CHUNK = 64

LANES = 128
MXU_COLS = 256
VMEM_LIMIT = 56 * 1024 * 1024

FFN_ROWS = 512
PROJ_ROWS = 256
MERGE_ROWS = 512
ATTN_Q = 256
ATTN_KV = 512
GLA_ROWS = 256
VT_ROWS = 80
NEG_BIG = -1e30


def _rms(xf, g):
    return xf * lax.rsqrt(jnp.mean(xf * xf, axis=-1, keepdims=True) + EPS) * g


def _dot(a, b):
    return jnp.dot(a, b, preferred_element_type=F32)


def _dot_nt(a, b):
    return lax.dot_general(a, b, (((1,), (1,)), ((), ())), preferred_element_type=F32)


def _dot_tn(a, b):
    return lax.dot_general(a, b, (((0,), (0,)), ((), ())), preferred_element_type=F32)


def _log_sigmoid(x):
    return jnp.minimum(x, 0.0) - jnp.log1p(jnp.exp(-jnp.abs(x)))


def _resident(shape):
    nd = len(shape)
    return pl.BlockSpec(shape, lambda *_: (0,) * nd, pipeline_mode=pl.Buffered(1))


def _params(sem):
    return pltpu.CompilerParams(dimension_semantics=sem, vmem_limit_bytes=VMEM_LIMIT)


def _ffn_kernel(x_ref, gpre_ref, wa_ref, wb_ref, wo_ref, gpost_ref, o_ref, h_ref):
    x = x_ref[...]
    xn = _rms(x, gpre_ref[...]).astype(BF16)
    dff = wa_ref.shape[1]
    for c0 in range(0, dff, MXU_COLS):
        a = _dot(xn, wa_ref[:, c0:c0 + MXU_COLS])
        b = _dot(xn, wb_ref[:, c0:c0 + MXU_COLS])
        h_ref[:, c0:c0 + MXU_COLS] = (a * jax.nn.sigmoid(a) * b).astype(BF16)
    y = _dot(h_ref[...], wo_ref[...])
    o_ref[...] = x + 0.5 * _rms(y, gpost_ref[...])


def _ffn(x, gpre, wa, wb, wo, gpost):
    t, d = x.shape
    dff = wa.shape[1]
    assert dff % MXU_COLS == 0
    tm = min(FFN_ROWS, t)
    assert t % tm == 0
    row = pl.BlockSpec((tm, d), lambda i: (i, 0))
    return pl.pallas_call(
        _ffn_kernel,
        grid=(t // tm,),
        in_specs=[row, _resident((1, d)), _resident((d, dff)), _resident((d, dff)),
                  _resident((dff, d)), _resident((1, d))],
        out_specs=row,
        out_shape=jax.ShapeDtypeStruct((t, d), F32),
        scratch_shapes=[pltpu.VMEM((tm, dff), BF16)],
        compiler_params=_params(("parallel",)),
        name="ffn",
    )(x, gpre, wa, wb, wo, gpost)


def _inproj_kernel(x_ref, g_ref, w_ref, qg_ref, wuq_ref, wuqr_ref, kvg_ref, wuk_ref, wuv_ref,
                   waf_ref, baf_ref, wab_ref, bab_ref, cos_ref, sin_ref,
                   q_ref, k_ref, v_ref, gq_ref, gk_ref, gv_ref, gf_ref, gb_ref, gog_ref, bg_ref,
                   *, segs, q_scale):
    u = _rms(x_ref[...], g_ref[...]).astype(BF16)

    def proj(name):
        c0, c1 = segs[name]
        return _dot(u, w_ref[:, c0:c1])

    cos = cos_ref[...]
    sin = sin_ref[...]
    nh = q_ref.shape[1] // LANES

    cqn = _rms(proj("cq"), qg_ref[...]).astype(BF16)
    q = _dot(cqn, wuq_ref[...])
    qr = _dot(cqn, wuqr_ref[...])
    q_ref[...] = ((q * jnp.tile(cos, (1, nh)) + qr * jnp.tile(sin, (1, nh))) * q_scale).astype(BF16)

    ckvn = _rms(proj("ckv"), kvg_ref[...]).astype(BF16)
    kpe = proj("kpe") * cos + proj("kper") * sin
    k_ref[...] = (_dot(ckvn, wuk_ref[...]) + jnp.tile(kpe, (1, nh))).astype(BF16)
    v_ref[...] = _dot(ckvn, wuv_ref[...]).astype(BF16)

    gq_ref[...] = proj("gq")
    gk_ref[...] = proj("gk")
    gv_ref[...] = proj("gv").astype(BF16)
    ga = proj("ga").astype(BF16)
    gf_ref[...] = _log_sigmoid(_dot(ga, waf_ref[...]) + baf_ref[...]) * (1.0 / GATE_NORM)
    gb_ref[...] = _log_sigmoid(_dot(ga, wab_ref[...]) + bab_ref[...]) * (1.0 / GATE_NORM)
    gog_ref[...] = proj("gog").astype(BF16)
    bg_ref[...] = proj("bg").astype(BF16)


def _inproj(x, lw, cos_tab, sin_tab, seq, segs, q_scale):
    t, d = x.shape
    tm = min(PROJ_ROWS, seq)
    assert seq % tm == 0
    per_seq = seq // tm
    qw = MLA_HEADS * LANES
    vw = MLA_HEADS * MLA_V_DIM
    gk_w = lw["waf"].shape[1]
    gv_w = lw["w_ob"].shape[0]
    row = lambda w: pl.BlockSpec((tm, w), lambda i: (i, 0))
    tab = pl.BlockSpec((tm, LANES), lambda i: (i % per_seq, 0))
    weights = [lw["mix_pre_g"], lw["w_all"], lw["q_norm_g"], lw["w_uq"], lw["w_uqr"], lw["kv_norm_g"],
               lw["w_uk"], lw["w_uv"], lw["waf"], lw["baf"], lw["wab"], lw["bab"]]
    out_widths = [(qw, BF16), (qw, BF16), (vw, BF16), (gk_w, F32), (gk_w, F32), (gv_w, BF16),
                  (gk_w, F32), (gk_w, F32), (gv_w, BF16), (2 * d, BF16)]
    return pl.pallas_call(
        functools.partial(_inproj_kernel, segs=segs, q_scale=q_scale),
        grid=(t // tm,),
        in_specs=[row(d)] + [_resident(w.shape) for w in weights] + [tab, tab],
        out_specs=[row(w) for w, _ in out_widths],
        out_shape=[jax.ShapeDtypeStruct((t, w), dt) for w, dt in out_widths],
        compiler_params=_params(("parallel",)),
        name="inproj",
    )(x, *weights, cos_tab, sin_tab)


def _attn_kernel(q_ref, k_ref, vt_ref, o_ref, *, tk):
    q = q_ref[...]
    tq = q.shape[0]
    seq = k_ref.shape[0]
    vd = o_ref.shape[0]

    def body(j, carry):
        m, acc = carry
        off = pl.multiple_of(j * tk, tk)
        st = _dot_nt(k_ref[pl.ds(off, tk), :], q)
        m_new = jnp.maximum(m, jnp.max(st, axis=0, keepdims=True))
        p = jnp.exp(st - m_new).astype(BF16)
        acc = jnp.exp(m - m_new) * acc + _dot(vt_ref[:, pl.ds(off, tk)], p)
        return m_new, acc

    m0 = jnp.full((1, tq), NEG_BIG, F32)
    acc0 = jnp.zeros((vt_ref.shape[0], tq), F32)
    _, acc = lax.fori_loop(0, seq // tk, body, (m0, acc0))
    o_ref[...] = acc[:vd] / acc[vd:vd + 1]


def _attention(q, k, vt):
    nb, seq, qw = q.shape
    nh = qw // LANES
    tq = min(ATTN_Q, seq)
    tk = min(ATTN_KV, seq)
    assert seq % tq == 0 and seq % tk == 0
    return pl.pallas_call(
        functools.partial(_attn_kernel, tk=tk),
        grid=(nb, nh, seq // tq),
        in_specs=[pl.BlockSpec((None, tq, LANES), lambda b, h, i: (b, i, h)),
                  pl.BlockSpec((None, seq, LANES), lambda b, h, i: (b, 0, h)),
                  pl.BlockSpec((None, None, VT_ROWS, seq), lambda b, h, i: (b, h, 0, 0))],
        out_specs=pl.BlockSpec((None, None, MLA_V_DIM, tq), lambda b, h, i: (b, h, 0, i)),
        out_shape=jax.ShapeDtypeStruct((nb, nh, MLA_V_DIM, seq), F32),
        compiler_params=_params(("parallel", "parallel", "arbitrary")),
        name="mla_attention",
    )(q, k, vt)


def _gla_kernel(q_ref, k_ref, v_ref, g_ref, o_ref, st_ref, *, reverse, scale):
    @pl.when(pl.program_id(1) == 0)
    def _():
        st_ref[...] = jnp.zeros_like(st_ref)

    c = GLA_CHUNK
    nh, dv, dk = st_ref.shape
    nchunks = q_ref.shape[0] // c
    row = lax.broadcasted_iota(jnp.int32, (c, c), 0)
    col = lax.broadcasted_iota(jnp.int32, (c, c), 1)
    keep = (col >= row) if reverse else (col <= row)
    tri = jnp.where(keep, 1.0, 0.0).astype(BF16)
    order = range(nchunks - 1, -1, -1) if reverse else range(nchunks)
    for ci in order:
        rows = slice(ci * c, (ci + 1) * c)
        g = g_ref[rows, :]
        g1 = g.astype(BF16)
        r1 = g - g1.astype(F32)
        g2 = r1.astype(BF16)
        g3 = (r1 - g2.astype(F32)).astype(BF16)
        cb = _dot(tri, g1) + _dot(tri, g2) + _dot(tri, g3)
        tot = cb[0:1, :] if reverse else cb[c - 1:c, :]
        q_t = (q_ref[rows, :] * scale) * jnp.exp(cb)
        kk = k_ref[rows, :]
        k_t = kk * jnp.exp(-cb)
        k_s = kk * jnp.exp(tot - cb)
        dec = jnp.exp(tot)
        v = v_ref[rows, :]
        for h in range(nh):
            ks = slice(h * dk, (h + 1) * dk)
            vs = slice(h * dv, (h + 1) * dv)
            qh = q_t[:, ks].astype(BF16)
            att = jnp.where(keep, _dot_nt(qh, k_t[:, ks].astype(BF16)), 0.0).astype(BF16)
            st = st_ref[h]
            o_ref[rows, vs] = _dot(att, v[:, vs]) + _dot_nt(qh, st.astype(BF16))
            st_ref[h] = st * dec[:, ks] + _dot_tn(v[:, vs], k_s[:, ks].astype(BF16))


def _gla_scan(gq, gk, gv, g, seq, reverse):
    t, kw = gq.shape
    vw = gv.shape[1]
    dk = kw // GLA_HEADS
    dv = vw // GLA_HEADS
    tb = min(GLA_ROWS, seq)
    assert seq % tb == 0 and tb % GLA_CHUNK == 0
    nblk = seq // tb
    if reverse:
        idx = lambda b, i: (b * nblk + (nblk - 1 - i), 0)
    else:
        idx = lambda b, i: (b * nblk + i, 0)
    return pl.pallas_call(
        functools.partial(_gla_kernel, reverse=reverse, scale=dk ** -0.5),
        grid=(t // seq, nblk),
        in_specs=[pl.BlockSpec((tb, kw), idx), pl.BlockSpec((tb, kw), idx),
                  pl.BlockSpec((tb, vw), idx), pl.BlockSpec((tb, kw), idx)],
        out_specs=pl.BlockSpec((tb, vw), idx),
        out_shape=jax.ShapeDtypeStruct((t, vw), F32),
        scratch_shapes=[pltpu.VMEM((GLA_HEADS, dv, dk), F32)],
        compiler_params=_params(("parallel", "arbitrary")),
        name="gla_bwd" if reverse else "gla_fwd",
    )(gq, gk, gv, g)


def _merge_kernel(x_ref, oa_ref, of_ref, ob_ref, gog_ref, bg_ref, ng_ref, woa_ref, wob_ref, wout_ref,
                  gpost_ref, o_ref):
    d = x_ref.shape[1]
    dv = ng_ref.shape[1]
    o = of_ref[...] + ob_ref[...]
    ng = ng_ref[...]
    on = jnp.concatenate([_rms(o[:, c0:c0 + dv], ng) for c0 in range(0, o.shape[1], dv)], axis=1)
    gog = gog_ref[...].astype(F32)
    o_b = _dot((on * (gog * jax.nn.sigmoid(gog))).astype(BF16), wob_ref[...])
    o_a = _dot(oa_ref[...], woa_ref[...])
    gate = jax.nn.sigmoid(bg_ref[...].astype(F32))
    merged = gate[:, :d] * o_a + gate[:, d:] * o_b
    y = _dot(merged.astype(BF16), wout_ref[...])
    o_ref[...] = x_ref[...] + _rms(y, gpost_ref[...])


def _merge(x, oa, of, ob, gog, bg, lw):
    t, d = x.shape
    tm = min(MERGE_ROWS, t)
    assert t % tm == 0
    row = lambda w: pl.BlockSpec((tm, w), lambda i: (i, 0))
    weights = [lw["gla_norm_g"], lw["w_oa"], lw["w_ob"], lw["w_out"], lw["mix_post_g"]]
    return pl.pallas_call(
        _merge_kernel,
        grid=(t // tm,),
        in_specs=[row(d), row(oa.shape[1]), row(of.shape[1]), row(ob.shape[1]), row(gog.shape[1]),
                  row(bg.shape[1])] + [_resident(w.shape) for w in weights],
        out_specs=row(d),
        out_shape=jax.ShapeDtypeStruct((t, d), F32),
        compiler_params=_params(("parallel",)),
        name="merge",
    )(x, oa, of, ob, gog, bg, *weights)


def _rotate_half_cols(w):
    half = ROPE_DIM // 2
    return jnp.concatenate([-w[..., half:], w[..., :half]], axis=-1)


def _head_slots(parts, lead):
    used = sum(p.shape[-1] for p in parts)
    pad = jnp.zeros(parts[0].shape[:-1] + (LANES - used,), parts[0].dtype)
    return jnp.concatenate(list(parts) + [pad], axis=-1).reshape(lead + (MLA_HEADS * LANES,))


def _prepare_weights(p):
    depth, d, _ = p["w_in"].shape
    q_lora = p["q_norm_g"].shape[-1]
    kv_lora = p["kv_norm_g"].shape[-1]
    gk_w = p["gla_ba_f"].shape[-1]
    gv_w = p["w_ob"].shape[1]
    sizes = (q_lora, kv_lora, ROPE_DIM, gk_w, gk_w, gv_w, 2 * GATE_RANK, gv_w, 2 * d)
    offs = [0]
    for s in sizes:
        offs.append(offs[-1] + s)
    cq, ckv, kpe, gq, gk, gv, ga, gog, bg = (p["w_in"][..., offs[i]:offs[i + 1]] for i in range(len(sizes)))

    def rope_slot(w):
        return jnp.pad(w, ((0, 0), (0, 0), (NOPE_DIM, LANES - NOPE_DIM - ROPE_DIM)))

    pieces = [("cq", cq), ("ckv", ckv), ("kpe", rope_slot(kpe)), ("kper", rope_slot(_rotate_half_cols(kpe))),
              ("gq", gq), ("gk", gk), ("gv", gv),
              ("ga", jnp.pad(ga, ((0, 0), (0, 0), (0, LANES - 2 * GATE_RANK)))), ("gog", gog), ("bg", bg)]
    segs, c0 = {}, 0
    for name, w in pieces:
        segs[name] = (c0, c0 + w.shape[-1])
        c0 += w.shape[-1]
    w_all = jnp.concatenate([w for _, w in pieces], axis=-1).astype(BF16)

    wq = p["w_uq"].reshape(depth, q_lora, MLA_HEADS, NOPE_DIM + ROPE_DIM)
    q_nope, q_pe = wq[..., :NOPE_DIM], wq[..., NOPE_DIM:]
    lead = (depth, q_lora)
    w_uq = _head_slots([q_nope, q_pe], lead).astype(BF16)
    w_uqr = _head_slots([jnp.zeros_like(q_nope), _rotate_half_cols(q_pe)], lead).astype(BF16)

    wkv = p["w_ukv"].reshape(depth, kv_lora, MLA_HEADS, NOPE_DIM + MLA_V_DIM)
    w_uk = _head_slots([wkv[..., :NOPE_DIM]], (depth, kv_lora)).astype(BF16)
    w_uv = wkv[..., NOPE_DIM:].reshape(depth, kv_lora, MLA_HEADS * MLA_V_DIM).astype(BF16)

    waf = jnp.pad(p["gla_wa2_f"], ((0, 0), (0, LANES - GATE_RANK), (0, 0))).astype(BF16)
    wab = jnp.pad(p["gla_wa2_b"], ((0, 0), (GATE_RANK, LANES - 2 * GATE_RANK), (0, 0))).astype(BF16)

    dff = p["ffn1_w_out"].shape[1]
    row = lambda g: g[:, None, :]
    stacked = dict(
        ffn1_pre_g=row(p["ffn1_pre_g"]), ffn1_wa=p["ffn1_w_in"][..., :dff].astype(BF16),
        ffn1_wb=p["ffn1_w_in"][..., dff:].astype(BF16), ffn1_wo=p["ffn1_w_out"].astype(BF16),
        ffn1_post_g=row(p["ffn1_post_g"]),
        ffn2_pre_g=row(p["ffn2_pre_g"]), ffn2_wa=p["ffn2_w_in"][..., :dff].astype(BF16),
        ffn2_wb=p["ffn2_w_in"][..., dff:].astype(BF16), ffn2_wo=p["ffn2_w_out"].astype(BF16),
        ffn2_post_g=row(p["ffn2_post_g"]),
        mix_pre_g=row(p["mix_pre_g"]), w_all=w_all, q_norm_g=row(p["q_norm_g"]), w_uq=w_uq, w_uqr=w_uqr,
        kv_norm_g=row(p["kv_norm_g"]), w_uk=w_uk, w_uv=w_uv, waf=waf, baf=row(p["gla_ba_f"]), wab=wab,
        bab=row(p["gla_ba_b"]), gla_norm_g=row(p["gla_norm_g"]), w_oa=p["w_oa"].astype(BF16),
        w_ob=p["w_ob"].astype(BF16), w_out=p["w_out"].astype(BF16), mix_post_g=row(p["mix_post_g"]),
    )
    return stacked, segs, depth


def _rope_tables(seq):
    inv_freq = 1.0 / (ROPE_THETA ** (jnp.arange(0, ROPE_DIM, 2, dtype=F32) / ROPE_DIM))
    ang = jnp.arange(seq, dtype=F32)[:, None] * inv_freq[None, :]
    cos, sin = jnp.cos(ang), jnp.sin(ang)
    pad = LANES - NOPE_DIM - ROPE_DIM
    cos_tab = jnp.concatenate([jnp.ones((seq, NOPE_DIM), F32), cos, cos, jnp.zeros((seq, pad), F32)], axis=1)
    sin_tab = jnp.concatenate([jnp.zeros((seq, NOPE_DIM), F32), sin, sin, jnp.zeros((seq, pad), F32)], axis=1)
    return cos_tab, sin_tab


def _values_transposed(v, nb, seq):
    vt = v.reshape(nb, seq, MLA_HEADS, MLA_V_DIM).transpose(0, 2, 3, 1)
    ones = jnp.ones((nb, MLA_HEADS, 1, seq), v.dtype)
    zeros = jnp.zeros((nb, MLA_HEADS, VT_ROWS - MLA_V_DIM - 1, seq), v.dtype)
    return jnp.concatenate([vt, ones, zeros], axis=2)


def kernel(x_prompt, x_sample, ffn1_pre_g, ffn1_w_in, ffn1_w_out, ffn1_post_g, mix_pre_g, w_in, q_norm_g, w_uq, kv_norm_g, w_ukv, w_oa, gla_wa2_f, gla_ba_f, gla_wa2_b, gla_ba_b, gla_norm_g, w_ob, w_out, mix_post_g, ffn2_pre_g, ffn2_w_in, ffn2_w_out, ffn2_post_g):
    params = dict(ffn1_pre_g=ffn1_pre_g, ffn1_w_in=ffn1_w_in, ffn1_w_out=ffn1_w_out, ffn1_post_g=ffn1_post_g,
                  mix_pre_g=mix_pre_g, w_in=w_in, q_norm_g=q_norm_g, w_uq=w_uq, kv_norm_g=kv_norm_g, w_ukv=w_ukv,
                  w_oa=w_oa, gla_wa2_f=gla_wa2_f, gla_ba_f=gla_ba_f, gla_wa2_b=gla_wa2_b, gla_ba_b=gla_ba_b,
                  gla_norm_g=gla_norm_g, w_ob=w_ob, w_out=w_out, mix_post_g=mix_post_g, ffn2_pre_g=ffn2_pre_g,
                  ffn2_w_in=ffn2_w_in, ffn2_w_out=ffn2_w_out, ffn2_post_g=ffn2_post_g)
    stacked, segs, depth = _prepare_weights(params)
    seq, d = x_prompt.shape[1], x_prompt.shape[2]
    assert x_sample.shape[1] == seq
    nb = x_prompt.shape[0] + x_sample.shape[0]
    x = jnp.concatenate([x_prompt, x_sample], axis=0).reshape(nb * seq, d)
    cos_tab, sin_tab = _rope_tables(seq)
    q_scale = (NOPE_DIM + ROPE_DIM) ** -0.5

    for l in range(depth):
        lw = {name: w[l] for name, w in stacked.items()}
        x = _ffn(x, lw["ffn1_pre_g"], lw["ffn1_wa"], lw["ffn1_wb"], lw["ffn1_wo"], lw["ffn1_post_g"])
        q, k, v, gq, gk, gv, gf, gb, gog, bg = _inproj(x, lw, cos_tab, sin_tab, seq, segs, q_scale)
        qw = q.shape[1]
        ot = _attention(q.reshape(nb, seq, qw), k.reshape(nb, seq, qw), _values_transposed(v, nb, seq))
        oa = ot.transpose(0, 3, 1, 2).reshape(nb * seq, MLA_HEADS * MLA_V_DIM).astype(BF16)
        of = _gla_scan(gq, gk, gv, gf, seq, reverse=False)
        ob = _gla_scan(gq, gk, gv, gb, seq, reverse=True)
        x = _merge(x, oa, of, ob, gog, bg, lw)
        x = _ffn(x, lw["ffn2_pre_g"], lw["ffn2_wa"], lw["ffn2_wb"], lw["ffn2_wo"], lw["ffn2_post_g"])

    x = x.reshape(nb, seq, d)
    return x[:x_prompt.shape[0]], x[x_prompt.shape[0]:]
```

```python
import functools

import jax
import jax.numpy as jnp
from jax import lax
from jax.experimental import pallas as pl
from jax.experimental.pallas import tpu as pltpu

F32 = jnp.float32
BF16 = jnp.bfloat16

EPS = 1e-6
MLA_HEADS = 8
NOPE_DIM = 64
ROPE_DIM = 32
MLA_V_DIM = 64
ROPE_THETA = 10000.0
GLA_HEADS = 4
GATE_RANK = 16
GATE_NORM = 16.0
GLA_CHUNK = 64

LANES = 128
MXU_COLS = 256
VMEM_LIMIT = 56 * 1024 * 1024

FFN_ROWS = 512
PROJ_ROWS = 256
MERGE_ROWS = 512
ATTN_Q = 2048
ATTN_KV = 1024
ATTN_KV_EXACT = 256
GLA_ROWS = 256
VT_ROWS = 80
NEG_BIG = -1e30
SHIFT_LIMIT = 16.0
LOG2E = 1.4426950408889634


def _rms(xf, g):
    return xf * lax.rsqrt(jnp.mean(xf * xf, axis=-1, keepdims=True) + EPS) * g


def _dot(a, b):
    return jnp.dot(a, b, preferred_element_type=F32)


def _dot_nt(a, b):
    return lax.dot_general(a, b, (((1,), (1,)), ((), ())), preferred_element_type=F32)


def _dot_tn(a, b):
    return lax.dot_general(a, b, (((0,), (0,)), ((), ())), preferred_element_type=F32)


def _log_sigmoid(x):
    return jnp.minimum(x, 0.0) - jnp.log1p(jnp.exp(-jnp.abs(x)))


def _resident(shape):
    nd = len(shape)
    return pl.BlockSpec(shape, lambda *_: (0,) * nd, pipeline_mode=pl.Buffered(1))


def _params(sem):
    return pltpu.CompilerParams(dimension_semantics=sem, vmem_limit_bytes=VMEM_LIMIT)


def _ffn_kernel(x_ref, gpre_ref, wa_ref, wb_ref, wo_ref, gpost_ref, o_ref, h_ref):
    x = x_ref[...]
    xn = _rms(x, gpre_ref[...]).astype(BF16)
    dff = wa_ref.shape[1]
    for c0 in range(0, dff, MXU_COLS):
        a = _dot(xn, wa_ref[:, c0:c0 + MXU_COLS])
        b = _dot(xn, wb_ref[:, c0:c0 + MXU_COLS])
        h_ref[:, c0:c0 + MXU_COLS] = (a * jax.nn.sigmoid(a) * b).astype(BF16)
    y = _dot(h_ref[...], wo_ref[...])
    o_ref[...] = x + 0.5 * _rms(y, gpost_ref[...])


def _ffn(x, gpre, wa, wb, wo, gpost):
    t, d = x.shape
    dff = wa.shape[1]
    assert dff % MXU_COLS == 0
    tm = min(FFN_ROWS, t)
    assert t % tm == 0
    row = pl.BlockSpec((tm, d), lambda i: (i, 0))
    return pl.pallas_call(
        _ffn_kernel,
        grid=(t // tm,),
        in_specs=[row, _resident((1, d)), _resident((d, dff)), _resident((d, dff)),
                  _resident((dff, d)), _resident((1, d))],
        out_specs=row,
        out_shape=jax.ShapeDtypeStruct((t, d), F32),
        scratch_shapes=[pltpu.VMEM((tm, dff), BF16)],
        compiler_params=_params(("parallel",)),
        name="ffn",
    )(x, gpre, wa, wb, wo, gpost)


def _inproj_kernel(x_ref, g_ref, w_ref, qg_ref, wuq_ref, wuqr_ref, kvg_ref, wuk_ref, wuv_ref,
                   waf_ref, baf_ref, wab_ref, bab_ref, cos_ref, sin_ref,
                   q_ref, k_ref, v_ref, gq_ref, gk_ref, gv_ref, gf_ref, gb_ref, gog_ref, bg_ref,
                   *, segs, q_scale):
    u = _rms(x_ref[...], g_ref[...]).astype(BF16)

    def proj(name):
        c0, c1 = segs[name]
        return _dot(u, w_ref[:, c0:c1])

    cos = cos_ref[...]
    sin = sin_ref[...]
    nh = q_ref.shape[1] // LANES

    cqn = _rms(proj("cq"), qg_ref[...]).astype(BF16)
    q = _dot(cqn, wuq_ref[...])
    qr = _dot(cqn, wuqr_ref[...])
    q_ref[...] = ((q * jnp.tile(cos, (1, nh)) + qr * jnp.tile(sin, (1, nh))) * q_scale).astype(BF16)

    ckvn = _rms(proj("ckv"), kvg_ref[...]).astype(BF16)
    kpe = proj("kpe") * cos + proj("kper") * sin
    k_ref[...] = (_dot(ckvn, wuk_ref[...]) + jnp.tile(kpe, (1, nh))).astype(BF16)
    v_ref[...] = _dot(ckvn, wuv_ref[...]).astype(BF16)

    gq_ref[...] = proj("gq")
    gk_ref[...] = proj("gk")
    gv_ref[...] = proj("gv").astype(BF16)
    ga = proj("ga").astype(BF16)
    gf_ref[...] = _log_sigmoid(_dot(ga, waf_ref[...]) + baf_ref[...]) * (1.0 / GATE_NORM)
    gb_ref[...] = _log_sigmoid(_dot(ga, wab_ref[...]) + bab_ref[...]) * (1.0 / GATE_NORM)
    gog_ref[...] = proj("gog").astype(BF16)
    bg_ref[...] = proj("bg").astype(BF16)


def _inproj(x, lw, cos_tab, sin_tab, seq, segs, q_scale):
    t, d = x.shape
    tm = min(PROJ_ROWS, seq)
    assert seq % tm == 0
    per_seq = seq // tm
    qw = MLA_HEADS * LANES
    vw = MLA_HEADS * MLA_V_DIM
    gk_w = lw["waf"].shape[1]
    gv_w = lw["w_ob"].shape[0]
    row = lambda w: pl.BlockSpec((tm, w), lambda i: (i, 0))
    tab = pl.BlockSpec((tm, LANES), lambda i: (i % per_seq, 0))
    weights = [lw["mix_pre_g"], lw["w_all"], lw["q_norm_g"], lw["w_uq"], lw["w_uqr"], lw["kv_norm_g"],
               lw["w_uk"], lw["w_uv"], lw["waf"], lw["baf"], lw["wab"], lw["bab"]]
    out_widths = [(qw, BF16), (qw, BF16), (vw, BF16), (gk_w, F32), (gk_w, F32), (gv_w, BF16),
                  (gk_w, F32), (gk_w, F32), (gv_w, BF16), (2 * d, BF16)]
    return pl.pallas_call(
        functools.partial(_inproj_kernel, segs=segs, q_scale=q_scale),
        grid=(t // tm,),
        in_specs=[row(d)] + [_resident(w.shape) for w in weights] + [tab, tab],
        out_specs=[row(w) for w, _ in out_widths],
        out_shape=[jax.ShapeDtypeStruct((t, w), dt) for w, dt in out_widths],
        compiler_params=_params(("parallel",)),
        name="inproj",
    )(x, *weights, cos_tab, sin_tab)


def _attn_exact(q_ref, k_ref, vt_ref, o_ref, *, tk):
    q = q_ref[...]
    tq = q.shape[0]
    vd = o_ref.shape[0]

    def body(j, carry):
        m, acc = carry
        off = pl.multiple_of(j * tk, tk)
        st = _dot_nt(k_ref[pl.ds(off, tk), :], q)
        m_new = jnp.maximum(m, jnp.max(st, axis=0, keepdims=True))
        p = jnp.exp2(st - m_new).astype(BF16)
        acc = jnp.exp2(m - m_new) * acc + _dot(vt_ref[:, pl.ds(off, tk)], p)
        return m_new, acc

    m0 = jnp.full((1, tq), NEG_BIG, F32)
    acc0 = jnp.zeros((vt_ref.shape[0], tq), F32)
    _, acc = lax.fori_loop(0, k_ref.shape[0] // tk, body, (m0, acc0))
    o_ref[...] = acc[:vd] / acc[vd:vd + 1]


def _attn_kernel(q_ref, k_ref, vt_ref, o_ref, p_ref, *, tk, tk_exact):
    q = q_ref[...]
    tq = q.shape[0]
    n = k_ref.shape[0] // tk
    vd = o_ref.shape[0]

    def scores(j):
        off = pl.multiple_of(j * tk, tk)
        return _dot_nt(k_ref[pl.ds(off, tk), :], q)

    def values(j, slot):
        off = pl.multiple_of(j * tk, tk)
        return _dot(vt_ref[:, pl.ds(off, tk)], p_ref[slot])

    def step(j, slot, carry):
        r_run, r_pend, r_acc, acc, risk = carry
        s = scores(j)
        p_ref[slot] = jnp.exp2(s - r_run).astype(BF16)
        c = jnp.max(s, axis=0, keepdims=True)
        risk = jnp.maximum(risk, c - r_run)
        acc = acc * jnp.exp2(r_acc - r_pend) + values(j - 1, 1 - slot)
        return jnp.maximum(r_run, c), r_run, r_pend, acc, risk

    s0 = scores(0)
    c0 = jnp.max(s0, axis=0, keepdims=True)
    p_ref[0] = jnp.exp2(s0 - c0).astype(BF16)
    carry = (c0, c0, c0, jnp.zeros((vt_ref.shape[0], tq), F32), jnp.zeros((1, tq), F32))

    def pair(t, carry):
        j = 1 + 2 * t
        return step(j + 1, 0, step(j, 1, carry))

    carry = lax.fori_loop(0, (n - 2) // 2, pair, carry)
    _, r_pend, r_acc, acc, risk = step(n - 1, 1, carry)
    acc = acc * jnp.exp2(r_acc - r_pend) + values(n - 1, 1)
    o_ref[...] = acc[:vd] / acc[vd:vd + 1]

    @pl.when(jnp.max(risk) > SHIFT_LIMIT)
    def _():
        _attn_exact(q_ref, k_ref, vt_ref, o_ref, tk=tk_exact)


def _attention(q, k, vt):
    nb, seq, qw = q.shape
    nh = qw // LANES
    tq = min(ATTN_Q, seq)
    tk = min(ATTN_KV, seq // 2)
    tk_exact = min(ATTN_KV_EXACT, seq)
    assert seq % tq == 0 and seq % (2 * tk) == 0 and seq % tk_exact == 0
    return pl.pallas_call(
        functools.partial(_attn_kernel, tk=tk, tk_exact=tk_exact),
        grid=(nb, nh, seq // tq),
        in_specs=[pl.BlockSpec((None, tq, LANES), lambda b, h, i: (b, i, h)),
                  pl.BlockSpec((None, seq, LANES), lambda b, h, i: (b, 0, h)),
                  pl.BlockSpec((None, None, VT_ROWS, seq), lambda b, h, i: (b, h, 0, 0))],
        out_specs=pl.BlockSpec((None, None, MLA_V_DIM, tq), lambda b, h, i: (b, h, 0, i)),
        out_shape=jax.ShapeDtypeStruct((nb, nh, MLA_V_DIM, seq), F32),
        scratch_shapes=[pltpu.VMEM((2, tk, tq), BF16)],
        compiler_params=_params(("parallel", "parallel", "arbitrary")),
        name="mla_attention",
    )(q, k, vt)


def _gla_kernel(q_ref, k_ref, v_ref, g_ref, o_ref, st_ref, *, reverse, scale):
    @pl.when(pl.program_id(1) == 0)
    def _():
        st_ref[...] = jnp.zeros_like(st_ref)

    c = GLA_CHUNK
    nh, dv, dk = st_ref.shape
    nchunks = q_ref.shape[0] // c
    row = lax.broadcasted_iota(jnp.int32, (c, c), 0)
    col = lax.broadcasted_iota(jnp.int32, (c, c), 1)
    keep = (col >= row) if reverse else (col <= row)
    tri = jnp.where(keep, 1.0, 0.0).astype(BF16)
    order = range(nchunks - 1, -1, -1) if reverse else range(nchunks)
    for ci in order:
        rows = slice(ci * c, (ci + 1) * c)
        g = g_ref[rows, :]
        g1 = g.astype(BF16)
        r1 = g - g1.astype(F32)
        g2 = r1.astype(BF16)
        g3 = (r1 - g2.astype(F32)).astype(BF16)
        cb = _dot(tri, g1) + _dot(tri, g2) + _dot(tri, g3)
        tot = cb[0:1, :] if reverse else cb[c - 1:c, :]
        q_t = (q_ref[rows, :] * scale) * jnp.exp(cb)
        kk = k_ref[rows, :]
        k_t = kk * jnp.exp(-cb)
        k_s = kk * jnp.exp(tot - cb)
        dec = jnp.exp(tot)
        v = v_ref[rows, :]
        for h in range(nh):
            ks = slice(h * dk, (h + 1) * dk)
            vs = slice(h * dv, (h + 1) * dv)
            qh = q_t[:, ks].astype(BF16)
            att = jnp.where(keep, _dot_nt(qh, k_t[:, ks].astype(BF16)), 0.0).astype(BF16)
            st = st_ref[h]
            o_ref[rows, vs] = (_dot(att, v[:, vs]) + _dot_nt(qh, st.astype(BF16))).astype(BF16)
            st_ref[h] = st * dec[:, ks] + _dot_tn(v[:, vs], k_s[:, ks].astype(BF16))


def _gla_scan(gq, gk, gv, g, seq, reverse):
    t, kw = gq.shape
    vw = gv.shape[1]
    dk = kw // GLA_HEADS
    dv = vw // GLA_HEADS
    tb = min(GLA_ROWS, seq)
    assert seq % tb == 0 and tb % GLA_CHUNK == 0
    nblk = seq // tb
    if reverse:
        idx = lambda b, i: (b * nblk + (nblk - 1 - i), 0)
    else:
        idx = lambda b, i: (b * nblk + i, 0)
    return pl.pallas_call(
        functools.partial(_gla_kernel, reverse=reverse, scale=dk ** -0.5),
        grid=(t // seq, nblk),
        in_specs=[pl.BlockSpec((tb, kw), idx), pl.BlockSpec((tb, kw), idx),
                  pl.BlockSpec((tb, vw), idx), pl.BlockSpec((tb, kw), idx)],
        out_specs=pl.BlockSpec((tb, vw), idx),
        out_shape=jax.ShapeDtypeStruct((t, vw), BF16),
        scratch_shapes=[pltpu.VMEM((GLA_HEADS, dv, dk), F32)],
        compiler_params=_params(("parallel", "arbitrary")),
        name="gla_bwd" if reverse else "gla_fwd",
    )(gq, gk, gv, g)


def _merge_kernel(x_ref, oa_ref, of_ref, ob_ref, gog_ref, bg_ref, ng_ref, woa_ref, wob_ref, wout_ref,
                  gpost_ref, o_ref):
    d = x_ref.shape[1]
    dv = ng_ref.shape[1]
    o = of_ref[...].astype(F32) + ob_ref[...].astype(F32)
    ng = ng_ref[...]
    on = jnp.concatenate([_rms(o[:, c0:c0 + dv], ng) for c0 in range(0, o.shape[1], dv)], axis=1)
    gog = gog_ref[...].astype(F32)
    o_b = _dot((on * (gog * jax.nn.sigmoid(gog))).astype(BF16), wob_ref[...])
    o_a = _dot(oa_ref[...], woa_ref[...])
    gate = jax.nn.sigmoid(bg_ref[...].astype(F32))
    merged = gate[:, :d] * o_a + gate[:, d:] * o_b
    y = _dot(merged.astype(BF16), wout_ref[...])
    o_ref[...] = x_ref[...] + _rms(y, gpost_ref[...])


def _merge(x, oa, of, ob, gog, bg, lw):
    t, d = x.shape
    tm = min(MERGE_ROWS, t)
    assert t % tm == 0
    row = lambda w: pl.BlockSpec((tm, w), lambda i: (i, 0))
    weights = [lw["gla_norm_g"], lw["w_oa"], lw["w_ob"], lw["w_out"], lw["mix_post_g"]]
    return pl.pallas_call(
        _merge_kernel,
        grid=(t // tm,),
        in_specs=[row(d), row(oa.shape[1]), row(of.shape[1]), row(ob.shape[1]), row(gog.shape[1]),
                  row(bg.shape[1])] + [_resident(w.shape) for w in weights],
        out_specs=row(d),
        out_shape=jax.ShapeDtypeStruct((t, d), F32),
        compiler_params=_params(("parallel",)),
        name="merge",
    )(x, oa, of, ob, gog, bg, *weights)


def _rotate_half_cols(w):
    half = ROPE_DIM // 2
    return jnp.concatenate([-w[..., half:], w[..., :half]], axis=-1)


def _head_slots(parts, lead):
    used = sum(p.shape[-1] for p in parts)
    pad = jnp.zeros(parts[0].shape[:-1] + (LANES - used,), parts[0].dtype)
    return jnp.concatenate(list(parts) + [pad], axis=-1).reshape(lead + (MLA_HEADS * LANES,))


def _prepare_weights(p):
    depth, d, _ = p["w_in"].shape
    q_lora = p["q_norm_g"].shape[-1]
    kv_lora = p["kv_norm_g"].shape[-1]
    gk_w = p["gla_ba_f"].shape[-1]
    gv_w = p["w_ob"].shape[1]
    sizes = (q_lora, kv_lora, ROPE_DIM, gk_w, gk_w, gv_w, 2 * GATE_RANK, gv_w, 2 * d)
    offs = [0]
    for s in sizes:
        offs.append(offs[-1] + s)
    cq, ckv, kpe, gq, gk, gv, ga, gog, bg = (p["w_in"][..., offs[i]:offs[i + 1]] for i in range(len(sizes)))

    def rope_slot(w):
        return jnp.pad(w, ((0, 0), (0, 0), (NOPE_DIM, LANES - NOPE_DIM - ROPE_DIM)))

    pieces = [("cq", cq), ("ckv", ckv), ("kpe", rope_slot(kpe)), ("kper", rope_slot(_rotate_half_cols(kpe))),
              ("gq", gq), ("gk", gk), ("gv", gv),
              ("ga", jnp.pad(ga, ((0, 0), (0, 0), (0, LANES - 2 * GATE_RANK)))), ("gog", gog), ("bg", bg)]
    segs, c0 = {}, 0
    for name, w in pieces:
        segs[name] = (c0, c0 + w.shape[-1])
        c0 += w.shape[-1]
    w_all = jnp.concatenate([w for _, w in pieces], axis=-1).astype(BF16)

    wq = p["w_uq"].reshape(depth, q_lora, MLA_HEADS, NOPE_DIM + ROPE_DIM)
    q_nope, q_pe = wq[..., :NOPE_DIM], wq[..., NOPE_DIM:]
    lead = (depth, q_lora)
    w_uq = _head_slots([q_nope, q_pe], lead).astype(BF16)
    w_uqr = _head_slots([jnp.zeros_like(q_nope), _rotate_half_cols(q_pe)], lead).astype(BF16)

    wkv = p["w_ukv"].reshape(depth, kv_lora, MLA_HEADS, NOPE_DIM + MLA_V_DIM)
    w_uk = _head_slots([wkv[..., :NOPE_DIM]], (depth, kv_lora)).astype(BF16)
    w_uv = wkv[..., NOPE_DIM:].reshape(depth, kv_lora, MLA_HEADS * MLA_V_DIM).astype(BF16)

    waf = jnp.pad(p["gla_wa2_f"], ((0, 0), (0, LANES - GATE_RANK), (0, 0))).astype(BF16)
    wab = jnp.pad(p["gla_wa2_b"], ((0, 0), (GATE_RANK, LANES - 2 * GATE_RANK), (0, 0))).astype(BF16)

    dff = p["ffn1_w_out"].shape[1]
    row = lambda g: g[:, None, :]
    stacked = dict(
        ffn1_pre_g=row(p["ffn1_pre_g"]), ffn1_wa=p["ffn1_w_in"][..., :dff].astype(BF16),
        ffn1_wb=p["ffn1_w_in"][..., dff:].astype(BF16), ffn1_wo=p["ffn1_w_out"].astype(BF16),
        ffn1_post_g=row(p["ffn1_post_g"]),
        ffn2_pre_g=row(p["ffn2_pre_g"]), ffn2_wa=p["ffn2_w_in"][..., :dff].astype(BF16),
        ffn2_wb=p["ffn2_w_in"][..., dff:].astype(BF16), ffn2_wo=p["ffn2_w_out"].astype(BF16),
        ffn2_post_g=row(p["ffn2_post_g"]),
        mix_pre_g=row(p["mix_pre_g"]), w_all=w_all, q_norm_g=row(p["q_norm_g"]), w_uq=w_uq, w_uqr=w_uqr,
        kv_norm_g=row(p["kv_norm_g"]), w_uk=w_uk, w_uv=w_uv, waf=waf, baf=row(p["gla_ba_f"]), wab=wab,
        bab=row(p["gla_ba_b"]), gla_norm_g=row(p["gla_norm_g"]), w_oa=p["w_oa"].astype(BF16),
        w_ob=p["w_ob"].astype(BF16), w_out=p["w_out"].astype(BF16), mix_post_g=row(p["mix_post_g"]),
    )
    return stacked, segs, depth


def _rope_tables(seq):
    inv_freq = 1.0 / (ROPE_THETA ** (jnp.arange(0, ROPE_DIM, 2, dtype=F32) / ROPE_DIM))
    ang = jnp.arange(seq, dtype=F32)[:, None] * inv_freq[None, :]
    cos, sin = jnp.cos(ang), jnp.sin(ang)
    pad = LANES - NOPE_DIM - ROPE_DIM
    cos_tab = jnp.concatenate([jnp.ones((seq, NOPE_DIM), F32), cos, cos, jnp.zeros((seq, pad), F32)], axis=1)
    sin_tab = jnp.concatenate([jnp.zeros((seq, NOPE_DIM), F32), sin, sin, jnp.zeros((seq, pad), F32)], axis=1)
    return cos_tab, sin_tab


def _values_transposed(v, nb, seq):
    vt = v.reshape(nb, seq, MLA_HEADS, MLA_V_DIM).transpose(0, 2, 3, 1)
    ones = jnp.ones((nb, MLA_HEADS, 1, seq), v.dtype)
    zeros = jnp.zeros((nb, MLA_HEADS, VT_ROWS - MLA_V_DIM - 1, seq), v.dtype)
    return jnp.concatenate([vt, ones, zeros], axis=2)


def kernel(x_prompt, x_sample, ffn1_pre_g, ffn1_w_in, ffn1_w_out, ffn1_post_g, mix_pre_g, w_in, q_norm_g, w_uq, kv_norm_g, w_ukv, w_oa, gla_wa2_f, gla_ba_f, gla_wa2_b, gla_ba_b, gla_norm_g, w_ob, w_out, mix_post_g, ffn2_pre_g, ffn2_w_in, ffn2_w_out, ffn2_post_g):
    params = dict(ffn1_pre_g=ffn1_pre_g, ffn1_w_in=ffn1_w_in, ffn1_w_out=ffn1_w_out, ffn1_post_g=ffn1_post_g,
                  mix_pre_g=mix_pre_g, w_in=w_in, q_norm_g=q_norm_g, w_uq=w_uq, kv_norm_g=kv_norm_g, w_ukv=w_ukv,
                  w_oa=w_oa, gla_wa2_f=gla_wa2_f, gla_ba_f=gla_ba_f, gla_wa2_b=gla_wa2_b, gla_ba_b=gla_ba_b,
                  gla_norm_g=gla_norm_g, w_ob=w_ob, w_out=w_out, mix_post_g=mix_post_g, ffn2_pre_g=ffn2_pre_g,
                  ffn2_w_in=ffn2_w_in, ffn2_w_out=ffn2_w_out, ffn2_post_g=ffn2_post_g)
    stacked, segs, depth = _prepare_weights(params)
    seq, d = x_prompt.shape[1], x_prompt.shape[2]
    assert x_sample.shape[1] == seq
    nb = x_prompt.shape[0] + x_sample.shape[0]
    x = jnp.concatenate([x_prompt, x_sample], axis=0).reshape(nb * seq, d)
    cos_tab, sin_tab = _rope_tables(seq)
    q_scale = LOG2E * (NOPE_DIM + ROPE_DIM) ** -0.5

    for l in range(depth):
        lw = {name: w[l] for name, w in stacked.items()}
        x = _ffn(x, lw["ffn1_pre_g"], lw["ffn1_wa"], lw["ffn1_wb"], lw["ffn1_wo"], lw["ffn1_post_g"])
        q, k, v, gq, gk, gv, gf, gb, gog, bg = _inproj(x, lw, cos_tab, sin_tab, seq, segs, q_scale)
        qw = q.shape[1]
        ot = _attention(q.reshape(nb, seq, qw), k.reshape(nb, seq, qw), _values_transposed(v, nb, seq))
        oa = ot.transpose(0, 3, 1, 2).reshape(nb * seq, MLA_HEADS * MLA_V_DIM).astype(BF16)
        of = _gla_scan(gq, gk, gv, gf, seq, reverse=False)
        ob = _gla_scan(gq, gk, gv, gb, seq, reverse=True)
        x = _merge(x, oa, of, ob, gog, bg, lw)
        x = _ffn(x, lw["ffn2_pre_g"], lw["ffn2_wa"], lw["ffn2_wb"], lw["ffn2_wo"], lw["ffn2_post_g"])

    x = x.reshape(nb, seq, d)
    return x[:x_prompt.shape[0]], x[x_prompt.shape[0]:]
```

```python
import functools

import jax
import jax.numpy as jnp
from jax import lax
from jax.experimental import pallas as pl
from jax.experimental.pallas import tpu as pltpu

F32 = jnp.float32
BF16 = jnp.bfloat16

EPS = 1e-6
MLA_HEADS = 8
NOPE_DIM = 64
ROPE_DIM = 32
MLA_V_DIM = 64
ROPE_THETA = 10000.0
GLA_HEADS = 4
GATE_RANK = 16
GATE_NORM = 16.0
GLA_CHUNK = 64

LANES = 128
MXU_COLS = 256
VMEM_LIMIT = 56 * 1024 * 1024

FFN_ROWS = 512
PROJ_ROWS = 512
MERGE_ROWS = 512
ATTN_Q = 2048
ATTN_KV = 1024
ATTN_KV_EXACT = 256
GLA_ROWS = 512
VT_ROWS = 80
NEG_BIG = -1e30
SHIFT_LIMIT = 16.0
LOG2E = 1.4426950408889634


def _rms(xf, g):
    return xf * lax.rsqrt(jnp.mean(xf * xf, axis=-1, keepdims=True) + EPS) * g


def _dot(a, b):
    return jnp.dot(a, b, preferred_element_type=F32)


def _dot_nt(a, b):
    return lax.dot_general(a, b, (((1,), (1,)), ((), ())), preferred_element_type=F32)


def _dot_tn(a, b):
    return lax.dot_general(a, b, (((0,), (0,)), ((), ())), preferred_element_type=F32)


def _log_sigmoid(x):
    return jnp.minimum(x, 0.0) - jnp.log1p(jnp.exp(-jnp.abs(x)))


def _resident(shape):
    nd = len(shape)
    return pl.BlockSpec(shape, lambda *_: (0,) * nd, pipeline_mode=pl.Buffered(1))


def _params(sem):
    return pltpu.CompilerParams(dimension_semantics=sem, vmem_limit_bytes=VMEM_LIMIT)


def _ffn_kernel(x_ref, gpre_ref, wa_ref, wb_ref, wo_ref, gpost_ref, o_ref, h_ref):
    x = x_ref[...]
    xn = _rms(x, gpre_ref[...]).astype(BF16)
    dff = wa_ref.shape[1]
    for c0 in range(0, dff, MXU_COLS):
        a = _dot(xn, wa_ref[:, c0:c0 + MXU_COLS])
        b = _dot(xn, wb_ref[:, c0:c0 + MXU_COLS])
        h_ref[:, c0:c0 + MXU_COLS] = (a * jax.nn.sigmoid(a) * b).astype(BF16)
    y = _dot(h_ref[...], wo_ref[...])
    o_ref[...] = x + 0.5 * _rms(y, gpost_ref[...])


def _ffn(x, gpre, wa, wb, wo, gpost):
    t, d = x.shape
    dff = wa.shape[1]
    assert dff % MXU_COLS == 0
    tm = min(FFN_ROWS, t)
    assert t % tm == 0
    row = pl.BlockSpec((tm, d), lambda i: (i, 0))
    return pl.pallas_call(
        _ffn_kernel,
        grid=(t // tm,),
        in_specs=[row, _resident((1, d)), _resident((d, dff)), _resident((d, dff)),
                  _resident((dff, d)), _resident((1, d))],
        out_specs=row,
        out_shape=jax.ShapeDtypeStruct((t, d), F32),
        scratch_shapes=[pltpu.VMEM((tm, dff), BF16)],
        compiler_params=_params(("parallel",)),
        name="ffn",
    )(x, gpre, wa, wb, wo, gpost)


def _inproj_kernel(x_ref, g_ref, w_ref, qg_ref, wuq_ref, wuqr_ref, kvg_ref, wuk_ref, wuvt_ref,
                   waf_ref, baf_ref, wab_ref, bab_ref, cos_ref, sin_ref,
                   q_ref, k_ref, vt_ref, gq_ref, gk_ref, gv_ref, gf_ref, gb_ref, gog_ref, bg_ref,
                   *, segs, q_scale):
    u = _rms(x_ref[...], g_ref[...]).astype(BF16)

    def proj(name):
        c0, c1 = segs[name]
        return _dot(u, w_ref[:, c0:c1])

    cos = cos_ref[...]
    sin = sin_ref[...]
    nh = q_ref.shape[1] // LANES

    cqn = _rms(proj("cq"), qg_ref[...]).astype(BF16)
    q = _dot(cqn, wuq_ref[...])
    qr = _dot(cqn, wuqr_ref[...])
    q_ref[...] = ((q * jnp.tile(cos, (1, nh)) + qr * jnp.tile(sin, (1, nh))) * q_scale).astype(BF16)

    ckvn = _rms(proj("ckv"), kvg_ref[...]).astype(BF16)
    kpe = proj("kpe") * cos + proj("kper") * sin
    k_ref[...] = (_dot(ckvn, wuk_ref[...]) + jnp.tile(kpe, (1, nh))).astype(BF16)
    tm = ckvn.shape[0]
    vt_ref[:, :MLA_V_DIM, :] = _dot_nt(wuvt_ref[...], ckvn).reshape(nh, MLA_V_DIM, tm).astype(BF16)
    pad_rows = vt_ref.shape[1] - MLA_V_DIM
    first = lax.broadcasted_iota(jnp.int32, (nh, pad_rows, tm), 1) == 0
    vt_ref[:, MLA_V_DIM:, :] = jnp.where(first, 1.0, 0.0).astype(BF16)

    gq_ref[...] = proj("gq")
    gk_ref[...] = proj("gk")
    gv_ref[...] = proj("gv").astype(BF16)
    ga = proj("ga").astype(BF16)
    gf_ref[...] = _log_sigmoid(_dot(ga, waf_ref[...]) + baf_ref[...]) * (1.0 / GATE_NORM)
    gb_ref[...] = _log_sigmoid(_dot(ga, wab_ref[...]) + bab_ref[...]) * (1.0 / GATE_NORM)
    gog_ref[...] = proj("gog").astype(BF16)
    bg_ref[...] = proj("bg").astype(BF16)


def _inproj(x, lw, cos_tab, sin_tab, seq, segs, q_scale):
    t, d = x.shape
    tm = min(PROJ_ROWS, seq)
    assert seq % tm == 0
    per_seq = seq // tm
    qw = MLA_HEADS * LANES
    gk_w = lw["waf"].shape[1]
    gv_w = lw["w_ob"].shape[0]
    row = lambda w: pl.BlockSpec((tm, w), lambda i: (i, 0))
    tab = pl.BlockSpec((tm, LANES), lambda i: (i % per_seq, 0))
    weights = [lw["mix_pre_g"], lw["w_all"], lw["q_norm_g"], lw["w_uq"], lw["w_uqr"], lw["kv_norm_g"],
               lw["w_uk"], lw["w_uvt"], lw["waf"], lw["baf"], lw["wab"], lw["bab"]]
    outs = [(qw, BF16), (qw, BF16), None, (gk_w, F32), (gk_w, F32), (gv_w, BF16),
            (gk_w, F32), (gk_w, F32), (gv_w, BF16), (2 * d, BF16)]
    vt_spec = pl.BlockSpec((None, MLA_HEADS, VT_ROWS, tm), lambda i: (i // per_seq, 0, 0, i % per_seq))
    vt_shape = jax.ShapeDtypeStruct((t // seq, MLA_HEADS, VT_ROWS, seq), BF16)
    return pl.pallas_call(
        functools.partial(_inproj_kernel, segs=segs, q_scale=q_scale),
        grid=(t // tm,),
        in_specs=[row(d)] + [_resident(w.shape) for w in weights] + [tab, tab],
        out_specs=[vt_spec if o is None else row(o[0]) for o in outs],
        out_shape=[vt_shape if o is None else jax.ShapeDtypeStruct((t, o[0]), o[1]) for o in outs],
        compiler_params=_params(("parallel",)),
        name="inproj",
    )(x, *weights, cos_tab, sin_tab)


def _attn_exact(q_ref, k_ref, vt_ref, o_ref, *, tk):
    q = q_ref[...]
    tq = q.shape[0]
    vd = o_ref.shape[0]

    def body(j, carry):
        m, acc = carry
        off = pl.multiple_of(j * tk, tk)
        st = _dot_nt(k_ref[pl.ds(off, tk), :], q)
        m_new = jnp.maximum(m, jnp.max(st, axis=0, keepdims=True))
        p = jnp.exp2(st - m_new).astype(BF16)
        acc = jnp.exp2(m - m_new) * acc + _dot(vt_ref[:, pl.ds(off, tk)], p)
        return m_new, acc

    m0 = jnp.full((1, tq), NEG_BIG, F32)
    acc0 = jnp.zeros((vt_ref.shape[0], tq), F32)
    _, acc = lax.fori_loop(0, k_ref.shape[0] // tk, body, (m0, acc0))
    o_ref[...] = (acc[:vd] / acc[vd:vd + 1]).astype(o_ref.dtype)


def _attn_kernel(q_ref, k_ref, vt_ref, o_ref, p_ref, *, tk, tk_exact):
    q = q_ref[...]
    tq = q.shape[0]
    n = k_ref.shape[0] // tk
    vd = o_ref.shape[0]

    def scores(j):
        off = pl.multiple_of(j * tk, tk)
        return _dot_nt(k_ref[pl.ds(off, tk), :], q)

    def values(j, slot):
        off = pl.multiple_of(j * tk, tk)
        return _dot(vt_ref[:, pl.ds(off, tk)], p_ref[slot])

    def step(j, slot, carry):
        r_run, r_pend, r_acc, acc, risk = carry
        s = scores(j)
        p_ref[slot] = jnp.exp2(s - r_run).astype(BF16)
        c = jnp.max(s, axis=0, keepdims=True)
        risk = jnp.maximum(risk, c - r_run)
        acc = acc * jnp.exp2(r_acc - r_pend) + values(j - 1, 1 - slot)
        return jnp.maximum(r_run, c), r_run, r_pend, acc, risk

    s0 = scores(0)
    c0 = jnp.max(s0, axis=0, keepdims=True)
    p_ref[0] = jnp.exp2(s0 - c0).astype(BF16)
    carry = (c0, c0, c0, jnp.zeros((vt_ref.shape[0], tq), F32), jnp.zeros((1, tq), F32))

    def pair(t, carry):
        j = 1 + 2 * t
        return step(j + 1, 0, step(j, 1, carry))

    carry = lax.fori_loop(0, (n - 2) // 2, pair, carry)
    _, r_pend, r_acc, acc, risk = step(n - 1, 1, carry)
    acc = acc * jnp.exp2(r_acc - r_pend) + values(n - 1, 1)
    o_ref[...] = (acc[:vd] / acc[vd:vd + 1]).astype(o_ref.dtype)

    @pl.when(jnp.max(risk) > SHIFT_LIMIT)
    def _():
        _attn_exact(q_ref, k_ref, vt_ref, o_ref, tk=tk_exact)


def _attention(q, k, vt):
    nb, seq, qw = q.shape
    nh = qw // LANES
    tq = min(ATTN_Q, seq)
    tk = min(ATTN_KV, seq // 2)
    tk_exact = min(ATTN_KV_EXACT, seq)
    assert seq % tq == 0 and seq % (2 * tk) == 0 and seq % tk_exact == 0
    return pl.pallas_call(
        functools.partial(_attn_kernel, tk=tk, tk_exact=tk_exact),
        grid=(nb, nh, seq // tq),
        in_specs=[pl.BlockSpec((None, tq, LANES), lambda b, h, i: (b, i, h)),
                  pl.BlockSpec((None, seq, LANES), lambda b, h, i: (b, 0, h)),
                  pl.BlockSpec((None, None, VT_ROWS, seq), lambda b, h, i: (b, h, 0, 0))],
        out_specs=pl.BlockSpec((None, None, MLA_V_DIM, tq), lambda b, h, i: (b, h, 0, i)),
        out_shape=jax.ShapeDtypeStruct((nb, nh, MLA_V_DIM, seq), BF16),
        scratch_shapes=[pltpu.VMEM((2, tk, tq), BF16)],
        compiler_params=_params(("parallel", "parallel", "arbitrary")),
        name="mla_attention",
    )(q, k, vt)


def _gla_kernel(q_ref, k_ref, v_ref, g_ref, o_ref, st_ref, *, reverse, scale):
    @pl.when(pl.program_id(1) == 0)
    def _():
        st_ref[...] = jnp.zeros_like(st_ref)

    c = GLA_CHUNK
    nh, dv, dk = st_ref.shape
    nchunks = q_ref.shape[0] // c
    row = lax.broadcasted_iota(jnp.int32, (c, c), 0)
    col = lax.broadcasted_iota(jnp.int32, (c, c), 1)
    keep = (col >= row) if reverse else (col <= row)
    tri = jnp.where(keep, 1.0, 0.0).astype(BF16)
    order = range(nchunks - 1, -1, -1) if reverse else range(nchunks)
    for ci in order:
        rows = slice(ci * c, (ci + 1) * c)
        g = g_ref[rows, :]
        g1 = g.astype(BF16)
        r1 = g - g1.astype(F32)
        g2 = r1.astype(BF16)
        g3 = (r1 - g2.astype(F32)).astype(BF16)
        cb = _dot(tri, g1) + _dot(tri, g2) + _dot(tri, g3)
        tot = cb[0:1, :] if reverse else cb[c - 1:c, :]
        q_t = (q_ref[rows, :] * scale) * jnp.exp(cb)
        kk = k_ref[rows, :]
        k_t = kk * jnp.exp(-cb)
        k_s = kk * jnp.exp(tot - cb)
        dec = jnp.exp(tot)
        v = v_ref[rows, :]
        for h in range(nh):
            ks = slice(h * dk, (h + 1) * dk)
            vs = slice(h * dv, (h + 1) * dv)
            qh = q_t[:, ks].astype(BF16)
            att = jnp.where(keep, _dot_nt(qh, k_t[:, ks].astype(BF16)), 0.0).astype(BF16)
            st = st_ref[h]
            o_ref[rows, vs] = (_dot(att, v[:, vs]) + _dot_nt(qh, st.astype(BF16))).astype(BF16)
            st_ref[h] = st * dec[:, ks] + _dot_tn(v[:, vs], k_s[:, ks].astype(BF16))


def _gla_scan(gq, gk, gv, g, seq, reverse):
    t, kw = gq.shape
    vw = gv.shape[1]
    dk = kw // GLA_HEADS
    dv = vw // GLA_HEADS
    tb = min(GLA_ROWS, seq)
    assert seq % tb == 0 and tb % GLA_CHUNK == 0
    nblk = seq // tb
    if reverse:
        idx = lambda b, i: (b * nblk + (nblk - 1 - i), 0)
    else:
        idx = lambda b, i: (b * nblk + i, 0)
    return pl.pallas_call(
        functools.partial(_gla_kernel, reverse=reverse, scale=dk ** -0.5),
        grid=(t // seq, nblk),
        in_specs=[pl.BlockSpec((tb, kw), idx), pl.BlockSpec((tb, kw), idx),
                  pl.BlockSpec((tb, vw), idx), pl.BlockSpec((tb, kw), idx)],
        out_specs=pl.BlockSpec((tb, vw), idx),
        out_shape=jax.ShapeDtypeStruct((t, vw), BF16),
        scratch_shapes=[pltpu.VMEM((GLA_HEADS, dv, dk), F32)],
        compiler_params=_params(("parallel", "arbitrary")),
        name="gla_bwd" if reverse else "gla_fwd",
    )(gq, gk, gv, g)


def _merge_kernel(x_ref, oat_ref, of_ref, ob_ref, gog_ref, bg_ref, ng_ref, woa_ref, wob_ref, wout_ref,
                  gpost_ref, o_ref):
    d = x_ref.shape[1]
    dv = ng_ref.shape[1]
    o = of_ref[...].astype(F32) + ob_ref[...].astype(F32)
    ng = ng_ref[...]
    on = jnp.concatenate([_rms(o[:, c0:c0 + dv], ng) for c0 in range(0, o.shape[1], dv)], axis=1)
    gog = gog_ref[...].astype(F32)
    o_b = _dot((on * (gog * jax.nn.sigmoid(gog))).astype(BF16), wob_ref[...])
    oat = oat_ref[...]
    o_a = _dot_tn(oat.reshape(oat.shape[0] * oat.shape[1], oat.shape[2]), woa_ref[...])
    gate = jax.nn.sigmoid(bg_ref[...].astype(F32))
    merged = gate[:, :d] * o_a + gate[:, d:] * o_b
    y = _dot(merged.astype(BF16), wout_ref[...])
    o_ref[...] = x_ref[...] + _rms(y, gpost_ref[...])


def _merge(x, oat, of, ob, gog, bg, lw):
    t, d = x.shape
    nb, nh, vd, seq = oat.shape
    tm = min(MERGE_ROWS, seq)
    assert seq % tm == 0
    per_seq = seq // tm
    row = lambda w: pl.BlockSpec((tm, w), lambda i: (i, 0))
    oat_spec = pl.BlockSpec((None, nh, vd, tm), lambda i: (i // per_seq, 0, 0, i % per_seq))
    weights = [lw["gla_norm_g"], lw["w_oa"], lw["w_ob"], lw["w_out"], lw["mix_post_g"]]
    return pl.pallas_call(
        _merge_kernel,
        grid=(t // tm,),
        in_specs=[row(d), oat_spec, row(of.shape[1]), row(ob.shape[1]), row(gog.shape[1]),
                  row(bg.shape[1])] + [_resident(w.shape) for w in weights],
        out_specs=row(d),
        out_shape=jax.ShapeDtypeStruct((t, d), F32),
        compiler_params=_params(("parallel",)),
        name="merge",
    )(x, oat, of, ob, gog, bg, *weights)


def _rotate_half_cols(w):
    half = ROPE_DIM // 2
    return jnp.concatenate([-w[..., half:], w[..., :half]], axis=-1)


def _head_slots(parts, lead):
    used = sum(p.shape[-1] for p in parts)
    pad = jnp.zeros(parts[0].shape[:-1] + (LANES - used,), parts[0].dtype)
    return jnp.concatenate(list(parts) + [pad], axis=-1).reshape(lead + (MLA_HEADS * LANES,))


def _prepare_weights(p):
    depth, d, _ = p["w_in"].shape
    q_lora = p["q_norm_g"].shape[-1]
    kv_lora = p["kv_norm_g"].shape[-1]
    gk_w = p["gla_ba_f"].shape[-1]
    gv_w = p["w_ob"].shape[1]
    sizes = (q_lora, kv_lora, ROPE_DIM, gk_w, gk_w, gv_w, 2 * GATE_RANK, gv_w, 2 * d)
    offs = [0]
    for s in sizes:
        offs.append(offs[-1] + s)
    cq, ckv, kpe, gq, gk, gv, ga, gog, bg = (p["w_in"][..., offs[i]:offs[i + 1]] for i in range(len(sizes)))

    def rope_slot(w):
        return jnp.pad(w, ((0, 0), (0, 0), (NOPE_DIM, LANES - NOPE_DIM - ROPE_DIM)))

    pieces = [("cq", cq), ("ckv", ckv), ("kpe", rope_slot(kpe)), ("kper", rope_slot(_rotate_half_cols(kpe))),
              ("gq", gq), ("gk", gk), ("gv", gv),
              ("ga", jnp.pad(ga, ((0, 0), (0, 0), (0, LANES - 2 * GATE_RANK)))), ("gog", gog), ("bg", bg)]
    segs, c0 = {}, 0
    for name, w in pieces:
        segs[name] = (c0, c0 + w.shape[-1])
        c0 += w.shape[-1]
    w_all = jnp.concatenate([w for _, w in pieces], axis=-1).astype(BF16)

    wq = p["w_uq"].reshape(depth, q_lora, MLA_HEADS, NOPE_DIM + ROPE_DIM)
    q_nope, q_pe = wq[..., :NOPE_DIM], wq[..., NOPE_DIM:]
    lead = (depth, q_lora)
    w_uq = _head_slots([q_nope, q_pe], lead).astype(BF16)
    w_uqr = _head_slots([jnp.zeros_like(q_nope), _rotate_half_cols(q_pe)], lead).astype(BF16)

    wkv = p["w_ukv"].reshape(depth, kv_lora, MLA_HEADS, NOPE_DIM + MLA_V_DIM)
    w_uk = _head_slots([wkv[..., :NOPE_DIM]], (depth, kv_lora)).astype(BF16)
    w_uvt = wkv[..., NOPE_DIM:].reshape(depth, kv_lora, MLA_HEADS * MLA_V_DIM).swapaxes(1, 2).astype(BF16)

    waf = jnp.pad(p["gla_wa2_f"], ((0, 0), (0, LANES - GATE_RANK), (0, 0))).astype(BF16)
    wab = jnp.pad(p["gla_wa2_b"], ((0, 0), (GATE_RANK, LANES - 2 * GATE_RANK), (0, 0))).astype(BF16)

    dff = p["ffn1_w_out"].shape[1]
    row = lambda g: g[:, None, :]
    stacked = dict(
        ffn1_pre_g=row(p["ffn1_pre_g"]), ffn1_wa=p["ffn1_w_in"][..., :dff].astype(BF16),
        ffn1_wb=p["ffn1_w_in"][..., dff:].astype(BF16), ffn1_wo=p["ffn1_w_out"].astype(BF16),
        ffn1_post_g=row(p["ffn1_post_g"]),
        ffn2_pre_g=row(p["ffn2_pre_g"]), ffn2_wa=p["ffn2_w_in"][..., :dff].astype(BF16),
        ffn2_wb=p["ffn2_w_in"][..., dff:].astype(BF16), ffn2_wo=p["ffn2_w_out"].astype(BF16),
        ffn2_post_g=row(p["ffn2_post_g"]),
        mix_pre_g=row(p["mix_pre_g"]), w_all=w_all, q_norm_g=row(p["q_norm_g"]), w_uq=w_uq, w_uqr=w_uqr,
        kv_norm_g=row(p["kv_norm_g"]), w_uk=w_uk, w_uvt=w_uvt, waf=waf, baf=row(p["gla_ba_f"]), wab=wab,
        bab=row(p["gla_ba_b"]), gla_norm_g=row(p["gla_norm_g"]), w_oa=p["w_oa"].astype(BF16),
        w_ob=p["w_ob"].astype(BF16), w_out=p["w_out"].astype(BF16), mix_post_g=row(p["mix_post_g"]),
    )
    return stacked, segs, depth


def _rope_tables(seq):
    inv_freq = 1.0 / (ROPE_THETA ** (jnp.arange(0, ROPE_DIM, 2, dtype=F32) / ROPE_DIM))
    ang = jnp.arange(seq, dtype=F32)[:, None] * inv_freq[None, :]
    cos, sin = jnp.cos(ang), jnp.sin(ang)
    pad = LANES - NOPE_DIM - ROPE_DIM
    cos_tab = jnp.concatenate([jnp.ones((seq, NOPE_DIM), F32), cos, cos, jnp.zeros((seq, pad), F32)], axis=1)
    sin_tab = jnp.concatenate([jnp.zeros((seq, NOPE_DIM), F32), sin, sin, jnp.zeros((seq, pad), F32)], axis=1)
    return cos_tab, sin_tab


def kernel(x_prompt, x_sample, ffn1_pre_g, ffn1_w_in, ffn1_w_out, ffn1_post_g, mix_pre_g, w_in, q_norm_g, w_uq, kv_norm_g, w_ukv, w_oa, gla_wa2_f, gla_ba_f, gla_wa2_b, gla_ba_b, gla_norm_g, w_ob, w_out, mix_post_g, ffn2_pre_g, ffn2_w_in, ffn2_w_out, ffn2_post_g):
    params = dict(ffn1_pre_g=ffn1_pre_g, ffn1_w_in=ffn1_w_in, ffn1_w_out=ffn1_w_out, ffn1_post_g=ffn1_post_g,
                  mix_pre_g=mix_pre_g, w_in=w_in, q_norm_g=q_norm_g, w_uq=w_uq, kv_norm_g=kv_norm_g, w_ukv=w_ukv,
                  w_oa=w_oa, gla_wa2_f=gla_wa2_f, gla_ba_f=gla_ba_f, gla_wa2_b=gla_wa2_b, gla_ba_b=gla_ba_b,
                  gla_norm_g=gla_norm_g, w_ob=w_ob, w_out=w_out, mix_post_g=mix_post_g, ffn2_pre_g=ffn2_pre_g,
                  ffn2_w_in=ffn2_w_in, ffn2_w_out=ffn2_w_out, ffn2_post_g=ffn2_post_g)
    stacked, segs, depth = _prepare_weights(params)
    seq, d = x_prompt.shape[1], x_prompt.shape[2]
    assert x_sample.shape[1] == seq
    nb = x_prompt.shape[0] + x_sample.shape[0]
    x = jnp.concatenate([x_prompt, x_sample], axis=0).reshape(nb * seq, d)
    cos_tab, sin_tab = _rope_tables(seq)
    q_scale = LOG2E * (NOPE_DIM + ROPE_DIM) ** -0.5

    for l in range(depth):
        lw = {name: w[l] for name, w in stacked.items()}
        x = _ffn(x, lw["ffn1_pre_g"], lw["ffn1_wa"], lw["ffn1_wb"], lw["ffn1_wo"], lw["ffn1_post_g"])
        q, k, vt, gq, gk, gv, gf, gb, gog, bg = _inproj(x, lw, cos_tab, sin_tab, seq, segs, q_scale)
        qw = q.shape[1]
        oat = _attention(q.reshape(nb, seq, qw), k.reshape(nb, seq, qw), vt)
        of = _gla_scan(gq, gk, gv, gf, seq, reverse=False)
        ob = _gla_scan(gq, gk, gv, gb, seq, reverse=True)
        x = _merge(x, oat, of, ob, gog, bg, lw)
        x = _ffn(x, lw["ffn2_pre_g"], lw["ffn2_wa"], lw["ffn2_wb"], lw["ffn2_wo"], lw["ffn2_post_g"])

    x = x.reshape(nb, seq, d)
    return x[:x_prompt.shape[0]], x[x_prompt.shape[0]:]
```

```python
import functools

import jax
import jax.numpy as jnp
from jax import lax
from jax.experimental import pallas as pl
from jax.experimental.pallas import tpu as pltpu

F32 = jnp.float32
BF16 = jnp.bfloat16

EPS = 1e-6
MLA_HEADS = 8
NOPE_DIM = 64
ROPE_DIM = 32
MLA_V_DIM = 64
ROPE_THETA = 10000.0
GLA_HEADS = 4
GATE_RANK = 16
GATE_NORM = 16.0
GLA_CHUNK = 64

LANES = 128
MXU_COLS = 256
VMEM_LIMIT = 56 * 1024 * 1024

FFN_ROWS = 512
PROJ_ROWS = 512
MERGE_ROWS = 512
ATTN_Q = 2048
ATTN_KV = 1024
ATTN_KV_EXACT = 256
GLA_ROWS = 512
VT_ROWS = 80
NEG_BIG = -1e30
SHIFT_LIMIT = 16.0
LOG2E = 1.4426950408889634


def _rms(xf, g):
    return xf * lax.rsqrt(jnp.mean(xf * xf, axis=-1, keepdims=True) + EPS) * g


def _dot(a, b):
    return jnp.dot(a, b, preferred_element_type=F32)


def _dot_nt(a, b):
    return lax.dot_general(a, b, (((1,), (1,)), ((), ())), preferred_element_type=F32)


def _dot_tn(a, b):
    return lax.dot_general(a, b, (((0,), (0,)), ((), ())), preferred_element_type=F32)


def _log_sigmoid(x):
    return jnp.minimum(x, 0.0) - jnp.log1p(jnp.exp(-jnp.abs(x)))


def _resident(shape):
    nd = len(shape)
    return pl.BlockSpec(shape, lambda *_: (0,) * nd, pipeline_mode=pl.Buffered(1))


def _params(sem):
    return pltpu.CompilerParams(dimension_semantics=sem, vmem_limit_bytes=VMEM_LIMIT)


def _ffn_kernel(x_ref, gpre_ref, wa_ref, wb_ref, wo_ref, gpost_ref, o_ref, h_ref):
    x = x_ref[...]
    xn = _rms(x, gpre_ref[...]).astype(BF16)
    dff = wa_ref.shape[1]
    for c0 in range(0, dff, MXU_COLS):
        a = _dot(xn, wa_ref[:, c0:c0 + MXU_COLS])
        b = _dot(xn, wb_ref[:, c0:c0 + MXU_COLS])
        h_ref[:, c0:c0 + MXU_COLS] = (a * jax.nn.sigmoid(a) * b).astype(BF16)
    y = _dot(h_ref[...], wo_ref[...])
    o_ref[...] = x + 0.5 * _rms(y, gpost_ref[...])


def _ffn(x, gpre, wa, wb, wo, gpost):
    t, d = x.shape
    dff = wa.shape[1]
    assert dff % MXU_COLS == 0
    tm = min(FFN_ROWS, t)
    assert t % tm == 0
    row = pl.BlockSpec((tm, d), lambda i: (i, 0))
    return pl.pallas_call(
        _ffn_kernel,
        grid=(t // tm,),
        in_specs=[row, _resident((1, d)), _resident((d, dff)), _resident((d, dff)),
                  _resident((dff, d)), _resident((1, d))],
        out_specs=row,
        out_shape=jax.ShapeDtypeStruct((t, d), F32),
        scratch_shapes=[pltpu.VMEM((tm, dff), BF16)],
        compiler_params=_params(("parallel",)),
        name="ffn",
    )(x, gpre, wa, wb, wo, gpost)


def _inproj_kernel(x_ref, g_ref, w_ref, qg_ref, wuqt_ref, wuqrt_ref, kvg_ref, wuk_ref, wuvt_ref,
                   waf_ref, baf_ref, wab_ref, bab_ref, cos_ref, sin_ref, cost_ref, sint_ref,
                   qt_ref, k_ref, vt_ref, gq_ref, gk_ref, gv_ref, gf_ref, gb_ref, gog_ref, bg_ref,
                   *, segs, q_scale):
    u = _rms(x_ref[...], g_ref[...]).astype(BF16)

    def proj(name):
        c0, c1 = segs[name]
        return _dot(u, w_ref[:, c0:c1])

    def proj_pair(first, second):
        c0, c1 = segs[first]
        both = _dot(u, w_ref[:, c0:segs[second][1]])
        return both[:, :c1 - c0], both[:, c1 - c0:]

    cos = cos_ref[...]
    sin = sin_ref[...]
    nh = k_ref.shape[1] // LANES
    ckv, kpe_raw = proj_pair("ckv", "kpe")
    kper_raw, ga = proj_pair("kper", "ga")

    cqn = _rms(proj("cq"), qg_ref[...]).astype(BF16)
    qt = _dot_nt(wuqt_ref[...], cqn)
    qrt = _dot_nt(wuqrt_ref[...], cqn)
    qt_ref[...] = ((qt * jnp.tile(cost_ref[...], (nh, 1)) + qrt * jnp.tile(sint_ref[...], (nh, 1)))
                   * q_scale).astype(BF16)

    ckvn = _rms(ckv, kvg_ref[...]).astype(BF16)
    kpe = kpe_raw * cos + kper_raw * sin
    k_ref[...] = (_dot(ckvn, wuk_ref[...]) + jnp.tile(kpe, (1, nh))).astype(BF16)
    tm = ckvn.shape[0]
    vt_ref[:, :MLA_V_DIM, :] = _dot_nt(wuvt_ref[...], ckvn).reshape(nh, MLA_V_DIM, tm).astype(BF16)
    pad_rows = vt_ref.shape[1] - MLA_V_DIM
    first = lax.broadcasted_iota(jnp.int32, (nh, pad_rows, tm), 1) == 0
    vt_ref[:, MLA_V_DIM:, :] = jnp.where(first, 1.0, 0.0).astype(BF16)

    gq_ref[...] = proj("gq")
    gk_ref[...] = proj("gk")
    gv_ref[...] = proj("gv").astype(BF16)
    ga = ga.astype(BF16)
    gf_ref[...] = _log_sigmoid(_dot(ga, waf_ref[...]) + baf_ref[...]) * (1.0 / GATE_NORM)
    gb_ref[...] = _log_sigmoid(_dot(ga, wab_ref[...]) + bab_ref[...]) * (1.0 / GATE_NORM)
    gog_ref[...] = proj("gog").astype(BF16)
    bg_ref[...] = proj("bg").astype(BF16)


def _inproj(x, lw, tables, seq, segs, q_scale):
    t, d = x.shape
    tm = min(PROJ_ROWS, seq)
    assert seq % tm == 0
    per_seq = seq // tm
    nb = t // seq
    qw = MLA_HEADS * LANES
    gk_w = lw["waf"].shape[1]
    gv_w = lw["w_ob"].shape[0]
    row = lambda w: pl.BlockSpec((tm, w), lambda i: (i, 0))
    tab = pl.BlockSpec((tm, LANES), lambda i: (i % per_seq, 0))
    tab_t = pl.BlockSpec((LANES, tm), lambda i: (0, i % per_seq))
    weights = [lw["mix_pre_g"], lw["w_all"], lw["q_norm_g"], lw["w_uqt"], lw["w_uqrt"], lw["kv_norm_g"],
               lw["w_uk"], lw["w_uvt"], lw["waf"], lw["baf"], lw["wab"], lw["bab"]]
    outs = ["qt", (qw, BF16), "vt", (gk_w, F32), (gk_w, F32), (gv_w, BF16),
            (gk_w, F32), (gk_w, F32), (gv_w, BF16), (2 * d, BF16)]
    special_spec = {
        "qt": pl.BlockSpec((None, qw, tm), lambda i: (i // per_seq, 0, i % per_seq)),
        "vt": pl.BlockSpec((None, MLA_HEADS, VT_ROWS, tm), lambda i: (i // per_seq, 0, 0, i % per_seq)),
    }
    special_shape = {
        "qt": jax.ShapeDtypeStruct((nb, qw, seq), BF16),
        "vt": jax.ShapeDtypeStruct((nb, MLA_HEADS, VT_ROWS, seq), BF16),
    }
    return pl.pallas_call(
        functools.partial(_inproj_kernel, segs=segs, q_scale=q_scale),
        grid=(t // tm,),
        in_specs=[row(d)] + [_resident(w.shape) for w in weights] + [tab, tab, tab_t, tab_t],
        out_specs=[special_spec[o] if isinstance(o, str) else row(o[0]) for o in outs],
        out_shape=[special_shape[o] if isinstance(o, str) else jax.ShapeDtypeStruct((t, o[0]), o[1])
                   for o in outs],
        compiler_params=_params(("parallel",)),
        name="inproj",
    )(x, *weights, *tables)


def _attn_exact(qt_ref, k_ref, vt_ref, o_ref, *, tk):
    qt = qt_ref[...]
    tq = qt.shape[1]
    vd = o_ref.shape[0]

    def body(j, carry):
        m, acc = carry
        off = pl.multiple_of(j * tk, tk)
        st = _dot(k_ref[pl.ds(off, tk), :], qt)
        m_new = jnp.maximum(m, jnp.max(st, axis=0, keepdims=True))
        p = jnp.exp2(st - m_new).astype(BF16)
        acc = jnp.exp2(m - m_new) * acc + _dot(vt_ref[:, pl.ds(off, tk)], p)
        return m_new, acc

    m0 = jnp.full((1, tq), NEG_BIG, F32)
    acc0 = jnp.zeros((vt_ref.shape[0], tq), F32)
    _, acc = lax.fori_loop(0, k_ref.shape[0] // tk, body, (m0, acc0))
    o_ref[...] = (acc[:vd] / acc[vd:vd + 1]).astype(o_ref.dtype)


def _attn_kernel(qt_ref, k_ref, vt_ref, o_ref, p_ref, *, tk, tk_exact):
    qt = qt_ref[...]
    tq = qt.shape[1]
    n = k_ref.shape[0] // tk
    vd = o_ref.shape[0]

    def scores(j):
        off = pl.multiple_of(j * tk, tk)
        return _dot(k_ref[pl.ds(off, tk), :], qt)

    def values(j, slot):
        off = pl.multiple_of(j * tk, tk)
        return _dot(vt_ref[:, pl.ds(off, tk)], p_ref[slot])

    def step(j, slot, carry):
        r_run, r_pend, r_acc, acc, risk = carry
        s = scores(j)
        p_ref[slot] = jnp.exp2(s - r_run).astype(BF16)
        c = jnp.max(s, axis=0, keepdims=True)
        risk = jnp.maximum(risk, c - r_run)
        acc = acc * jnp.exp2(r_acc - r_pend) + values(j - 1, 1 - slot)
        return jnp.maximum(r_run, c), r_run, r_pend, acc, risk

    s0 = scores(0)
    c0 = jnp.max(s0, axis=0, keepdims=True)
    p_ref[0] = jnp.exp2(s0 - c0).astype(BF16)
    carry = (c0, c0, c0, jnp.zeros((vt_ref.shape[0], tq), F32), jnp.zeros((1, tq), F32))

    def pair(t, carry):
        j = 1 + 2 * t
        return step(j + 1, 0, step(j, 1, carry))

    carry = lax.fori_loop(0, (n - 2) // 2, pair, carry)
    _, r_pend, r_acc, acc, risk = step(n - 1, 1, carry)
    acc = acc * jnp.exp2(r_acc - r_pend) + values(n - 1, 1)
    o_ref[...] = (acc[:vd] / acc[vd:vd + 1]).astype(o_ref.dtype)

    @pl.when(jnp.max(risk) > SHIFT_LIMIT)
    def _():
        _attn_exact(qt_ref, k_ref, vt_ref, o_ref, tk=tk_exact)


def _attention(qt, k, vt):
    nb, seq, qw = k.shape
    nh = qw // LANES
    tq = min(ATTN_Q, seq)
    tk = min(ATTN_KV, seq // 2)
    tk_exact = min(ATTN_KV_EXACT, seq)
    assert seq % tq == 0 and seq % (2 * tk) == 0 and seq % tk_exact == 0
    return pl.pallas_call(
        functools.partial(_attn_kernel, tk=tk, tk_exact=tk_exact),
        grid=(nb, nh, seq // tq),
        in_specs=[pl.BlockSpec((None, LANES, tq), lambda b, h, i: (b, h, i)),
                  pl.BlockSpec((None, seq, LANES), lambda b, h, i: (b, 0, h)),
                  pl.BlockSpec((None, None, VT_ROWS, seq), lambda b, h, i: (b, h, 0, 0))],
        out_specs=pl.BlockSpec((None, None, MLA_V_DIM, tq), lambda b, h, i: (b, h, 0, i)),
        out_shape=jax.ShapeDtypeStruct((nb, nh, MLA_V_DIM, seq), BF16),
        scratch_shapes=[pltpu.VMEM((2, tk, tq), BF16)],
        compiler_params=_params(("parallel", "parallel", "arbitrary")),
        name="mla_attention",
    )(qt, k, vt)


def _gla_kernel(q_ref, k_ref, v_ref, g_ref, o_ref, st_ref, *, reverse, scale):
    @pl.when(pl.program_id(1) == 0)
    def _():
        st_ref[...] = jnp.zeros_like(st_ref)

    c = GLA_CHUNK
    nh, dv, dk = st_ref.shape
    nchunks = q_ref.shape[0] // c
    row = lax.broadcasted_iota(jnp.int32, (c, c), 0)
    col = lax.broadcasted_iota(jnp.int32, (c, c), 1)
    keep = (col >= row) if reverse else (col <= row)
    tri = jnp.where(keep, 1.0, 0.0).astype(BF16)
    order = range(nchunks - 1, -1, -1) if reverse else range(nchunks)
    for ci in order:
        rows = slice(ci * c, (ci + 1) * c)
        g = g_ref[rows, :]
        g1 = g.astype(BF16)
        r1 = g - g1.astype(F32)
        g2 = r1.astype(BF16)
        g3 = (r1 - g2.astype(F32)).astype(BF16)
        cb = _dot(tri, g1) + _dot(tri, g2) + _dot(tri, g3)
        tot = cb[0:1, :] if reverse else cb[c - 1:c, :]
        q_t = (q_ref[rows, :] * scale) * jnp.exp(cb)
        kk = k_ref[rows, :]
        k_t = kk * jnp.exp(-cb)
        k_s = kk * jnp.exp(tot - cb)
        dec = jnp.exp(tot)
        v = v_ref[rows, :]
        for h in range(nh):
            ks = slice(h * dk, (h + 1) * dk)
            vs = slice(h * dv, (h + 1) * dv)
            qh = q_t[:, ks].astype(BF16)
            att = jnp.where(keep, _dot_nt(qh, k_t[:, ks].astype(BF16)), 0.0).astype(BF16)
            st = st_ref[h]
            o_ref[rows, vs] = (_dot(att, v[:, vs]) + _dot_nt(qh, st.astype(BF16))).astype(BF16)
            st_ref[h] = st * dec[:, ks] + _dot_tn(v[:, vs], k_s[:, ks].astype(BF16))


def _gla_scan(gq, gk, gv, g, seq, reverse):
    t, kw = gq.shape
    vw = gv.shape[1]
    dk = kw // GLA_HEADS
    dv = vw // GLA_HEADS
    tb = min(GLA_ROWS, seq)
    assert seq % tb == 0 and tb % GLA_CHUNK == 0
    nblk = seq // tb
    if reverse:
        idx = lambda b, i: (b * nblk + (nblk - 1 - i), 0)
    else:
        idx = lambda b, i: (b * nblk + i, 0)
    return pl.pallas_call(
        functools.partial(_gla_kernel, reverse=reverse, scale=dk ** -0.5),
        grid=(t // seq, nblk),
        in_specs=[pl.BlockSpec((tb, kw), idx), pl.BlockSpec((tb, kw), idx),
                  pl.BlockSpec((tb, vw), idx), pl.BlockSpec((tb, kw), idx)],
        out_specs=pl.BlockSpec((tb, vw), idx),
        out_shape=jax.ShapeDtypeStruct((t, vw), BF16),
        scratch_shapes=[pltpu.VMEM((GLA_HEADS, dv, dk), F32)],
        compiler_params=_params(("parallel", "arbitrary")),
        name="gla_bwd" if reverse else "gla_fwd",
    )(gq, gk, gv, g)


def _merge_kernel(x_ref, oat_ref, of_ref, ob_ref, gog_ref, bg_ref, ng_ref, woa_ref, wob_ref, wout_ref,
                  gpost_ref, o_ref):
    d = x_ref.shape[1]
    dv = ng_ref.shape[1]
    o = of_ref[...].astype(F32) + ob_ref[...].astype(F32)
    ng = ng_ref[...]
    on = jnp.concatenate([_rms(o[:, c0:c0 + dv], ng) for c0 in range(0, o.shape[1], dv)], axis=1)
    gog = gog_ref[...].astype(F32)
    o_b = _dot((on * (gog * jax.nn.sigmoid(gog))).astype(BF16), wob_ref[...])
    oat = oat_ref[...]
    o_a = _dot_tn(oat.reshape(oat.shape[0] * oat.shape[1], oat.shape[2]), woa_ref[...])
    gate = jax.nn.sigmoid(bg_ref[...].astype(F32))
    merged = gate[:, :d] * o_a + gate[:, d:] * o_b
    y = _dot(merged.astype(BF16), wout_ref[...])
    o_ref[...] = x_ref[...] + _rms(y, gpost_ref[...])


def _merge(x, oat, of, ob, gog, bg, lw):
    t, d = x.shape
    nb, nh, vd, seq = oat.shape
    tm = min(MERGE_ROWS, seq)
    assert seq % tm == 0
    per_seq = seq // tm
    row = lambda w: pl.BlockSpec((tm, w), lambda i: (i, 0))
    oat_spec = pl.BlockSpec((None, nh, vd, tm), lambda i: (i // per_seq, 0, 0, i % per_seq))
    weights = [lw["gla_norm_g"], lw["w_oa"], lw["w_ob"], lw["w_out"], lw["mix_post_g"]]
    return pl.pallas_call(
        _merge_kernel,
        grid=(t // tm,),
        in_specs=[row(d), oat_spec, row(of.shape[1]), row(ob.shape[1]), row(gog.shape[1]),
                  row(bg.shape[1])] + [_resident(w.shape) for w in weights],
        out_specs=row(d),
        out_shape=jax.ShapeDtypeStruct((t, d), F32),
        compiler_params=_params(("parallel",)),
        name="merge",
    )(x, oat, of, ob, gog, bg, *weights)


def _rotate_half_cols(w):
    half = ROPE_DIM // 2
    return jnp.concatenate([-w[..., half:], w[..., :half]], axis=-1)


def _head_slots(parts, lead):
    used = sum(p.shape[-1] for p in parts)
    pad = jnp.zeros(parts[0].shape[:-1] + (LANES - used,), parts[0].dtype)
    return jnp.concatenate(list(parts) + [pad], axis=-1).reshape(lead + (MLA_HEADS * LANES,))


def _prepare_weights(p):
    depth, d, _ = p["w_in"].shape
    q_lora = p["q_norm_g"].shape[-1]
    kv_lora = p["kv_norm_g"].shape[-1]
    gk_w = p["gla_ba_f"].shape[-1]
    gv_w = p["w_ob"].shape[1]
    sizes = (q_lora, kv_lora, ROPE_DIM, gk_w, gk_w, gv_w, 2 * GATE_RANK, gv_w, 2 * d)
    offs = [0]
    for s in sizes:
        offs.append(offs[-1] + s)
    cq, ckv, kpe, gq, gk, gv, ga, gog, bg = (p["w_in"][..., offs[i]:offs[i + 1]] for i in range(len(sizes)))

    def rope_slot(w):
        return jnp.pad(w, ((0, 0), (0, 0), (NOPE_DIM, LANES - NOPE_DIM - ROPE_DIM)))

    pieces = [("cq", cq), ("ckv", ckv), ("kpe", rope_slot(kpe)), ("kper", rope_slot(_rotate_half_cols(kpe))),
              ("ga", jnp.pad(ga, ((0, 0), (0, 0), (0, LANES - 2 * GATE_RANK)))),
              ("gq", gq), ("gk", gk), ("gv", gv), ("gog", gog), ("bg", bg)]
    segs, c0 = {}, 0
    for name, w in pieces:
        segs[name] = (c0, c0 + w.shape[-1])
        c0 += w.shape[-1]
    w_all = jnp.concatenate([w for _, w in pieces], axis=-1).astype(BF16)

    wq = p["w_uq"].reshape(depth, q_lora, MLA_HEADS, NOPE_DIM + ROPE_DIM)
    q_nope, q_pe = wq[..., :NOPE_DIM], wq[..., NOPE_DIM:]
    lead = (depth, q_lora)
    w_uqt = _head_slots([q_nope, q_pe], lead).swapaxes(1, 2).astype(BF16)
    w_uqrt = _head_slots([jnp.zeros_like(q_nope), _rotate_half_cols(q_pe)], lead).swapaxes(1, 2).astype(BF16)

    wkv = p["w_ukv"].reshape(depth, kv_lora, MLA_HEADS, NOPE_DIM + MLA_V_DIM)
    w_uk = _head_slots([wkv[..., :NOPE_DIM]], (depth, kv_lora)).astype(BF16)
    w_uvt = wkv[..., NOPE_DIM:].reshape(depth, kv_lora, MLA_HEADS * MLA_V_DIM).swapaxes(1, 2).astype(BF16)

    waf = jnp.pad(p["gla_wa2_f"], ((0, 0), (0, LANES - GATE_RANK), (0, 0))).astype(BF16)
    wab = jnp.pad(p["gla_wa2_b"], ((0, 0), (GATE_RANK, LANES - 2 * GATE_RANK), (0, 0))).astype(BF16)

    dff = p["ffn1_w_out"].shape[1]
    row = lambda g: g[:, None, :]
    stacked = dict(
        ffn1_pre_g=row(p["ffn1_pre_g"]), ffn1_wa=p["ffn1_w_in"][..., :dff].astype(BF16),
        ffn1_wb=p["ffn1_w_in"][..., dff:].astype(BF16), ffn1_wo=p["ffn1_w_out"].astype(BF16),
        ffn1_post_g=row(p["ffn1_post_g"]),
        ffn2_pre_g=row(p["ffn2_pre_g"]), ffn2_wa=p["ffn2_w_in"][..., :dff].astype(BF16),
        ffn2_wb=p["ffn2_w_in"][..., dff:].astype(BF16), ffn2_wo=p["ffn2_w_out"].astype(BF16),
        ffn2_post_g=row(p["ffn2_post_g"]),
        mix_pre_g=row(p["mix_pre_g"]), w_all=w_all, q_norm_g=row(p["q_norm_g"]), w_uqt=w_uqt, w_uqrt=w_uqrt,
        kv_norm_g=row(p["kv_norm_g"]), w_uk=w_uk, w_uvt=w_uvt, waf=waf, baf=row(p["gla_ba_f"]), wab=wab,
        bab=row(p["gla_ba_b"]), gla_norm_g=row(p["gla_norm_g"]), w_oa=p["w_oa"].astype(BF16),
        w_ob=p["w_ob"].astype(BF16), w_out=p["w_out"].astype(BF16), mix_post_g=row(p["mix_post_g"]),
    )
    return stacked, segs, depth


def _rope_tables(seq):
    inv_freq = 1.0 / (ROPE_THETA ** (jnp.arange(0, ROPE_DIM, 2, dtype=F32) / ROPE_DIM))
    ang = jnp.arange(seq, dtype=F32)[:, None] * inv_freq[None, :]
    cos, sin = jnp.cos(ang), jnp.sin(ang)
    pad = LANES - NOPE_DIM - ROPE_DIM
    cos_tab = jnp.concatenate([jnp.ones((seq, NOPE_DIM), F32), cos, cos, jnp.zeros((seq, pad), F32)], axis=1)
    sin_tab = jnp.concatenate([jnp.zeros((seq, NOPE_DIM), F32), sin, sin, jnp.zeros((seq, pad), F32)], axis=1)
    return cos_tab, sin_tab, cos_tab.T, sin_tab.T


def kernel(x_prompt, x_sample, ffn1_pre_g, ffn1_w_in, ffn1_w_out, ffn1_post_g, mix_pre_g, w_in, q_norm_g, w_uq, kv_norm_g, w_ukv, w_oa, gla_wa2_f, gla_ba_f, gla_wa2_b, gla_ba_b, gla_norm_g, w_ob, w_out, mix_post_g, ffn2_pre_g, ffn2_w_in, ffn2_w_out, ffn2_post_g):
    params = dict(ffn1_pre_g=ffn1_pre_g, ffn1_w_in=ffn1_w_in, ffn1_w_out=ffn1_w_out, ffn1_post_g=ffn1_post_g,
                  mix_pre_g=mix_pre_g, w_in=w_in, q_norm_g=q_norm_g, w_uq=w_uq, kv_norm_g=kv_norm_g, w_ukv=w_ukv,
                  w_oa=w_oa, gla_wa2_f=gla_wa2_f, gla_ba_f=gla_ba_f, gla_wa2_b=gla_wa2_b, gla_ba_b=gla_ba_b,
                  gla_norm_g=gla_norm_g, w_ob=w_ob, w_out=w_out, mix_post_g=mix_post_g, ffn2_pre_g=ffn2_pre_g,
                  ffn2_w_in=ffn2_w_in, ffn2_w_out=ffn2_w_out, ffn2_post_g=ffn2_post_g)
    stacked, segs, depth = _prepare_weights(params)
    seq, d = x_prompt.shape[1], x_prompt.shape[2]
    assert x_sample.shape[1] == seq
    nb = x_prompt.shape[0] + x_sample.shape[0]
    x = jnp.concatenate([x_prompt, x_sample], axis=0).reshape(nb * seq, d)
    tables = _rope_tables(seq)
    q_scale = LOG2E * (NOPE_DIM + ROPE_DIM) ** -0.5

    for l in range(depth):
        lw = {name: w[l] for name, w in stacked.items()}
        x = _ffn(x, lw["ffn1_pre_g"], lw["ffn1_wa"], lw["ffn1_wb"], lw["ffn1_wo"], lw["ffn1_post_g"])
        qt, k, vt, gq, gk, gv, gf, gb, gog, bg = _inproj(x, lw, tables, seq, segs, q_scale)
        oat = _attention(qt, k.reshape(nb, seq, k.shape[1]), vt)
        of = _gla_scan(gq, gk, gv, gf, seq, reverse=False)
        ob = _gla_scan(gq, gk, gv, gb, seq, reverse=True)
        x = _merge(x, oat, of, ob, gog, bg, lw)
        x = _ffn(x, lw["ffn2_pre_g"], lw["ffn2_wa"], lw["ffn2_wb"], lw["ffn2_wo"], lw["ffn2_post_g"])

    x = x.reshape(nb, seq, d)
    return x[:x_prompt.shape[0]], x[x_prompt.shape[0]:]
```

```python
import functools

import jax
import jax.numpy as jnp
from jax import lax
from jax.experimental import pallas as pl
from jax.experimental.pallas import tpu as pltpu

F32 = jnp.float32
BF16 = jnp.bfloat16

EPS = 1e-6
MLA_HEADS = 8
NOPE_DIM = 64
ROPE_DIM = 32
MLA_V_DIM = 64
ROPE_THETA = 10000.0
GLA_HEADS = 4
GATE_RANK = 16
GATE_NORM = 16.0
GLA_CHUNK = 64

LANES = 128
MXU_COLS = 256
VMEM_LIMIT = 56 * 1024 * 1024

FFN_ROWS = 512
PROJ_ROWS = 512
MERGE_ROWS = 512
ATTN_Q = 2048
ATTN_KV = 512
ATTN_UNROLL = 10
ATTN_KV_EXACT = 256
GLA_ROWS = 512
VT_ROWS = 80
NEG_BIG = -1e30
SHIFT_LIMIT = 16.0
LOG2E = 1.4426950408889634


def _rms(xf, g):
    return xf * lax.rsqrt(jnp.mean(xf * xf, axis=-1, keepdims=True) + EPS) * g


def _dot(a, b):
    return jnp.dot(a, b, preferred_element_type=F32)


def _dot_nt(a, b):
    return lax.dot_general(a, b, (((1,), (1,)), ((), ())), preferred_element_type=F32)


def _dot_tn(a, b):
    return lax.dot_general(a, b, (((0,), (0,)), ((), ())), preferred_element_type=F32)


def _log_sigmoid(x):
    return jnp.minimum(x, 0.0) - jnp.log1p(jnp.exp(-jnp.abs(x)))


def _resident(shape):
    nd = len(shape)
    return pl.BlockSpec(shape, lambda *_: (0,) * nd, pipeline_mode=pl.Buffered(1))


def _params(sem):
    return pltpu.CompilerParams(dimension_semantics=sem, vmem_limit_bytes=VMEM_LIMIT)


def _ffn_kernel(x_ref, gpre_ref, wa_ref, wb_ref, wo_ref, gpost_ref, o_ref, h_ref):
    x = x_ref[...]
    xn = _rms(x, gpre_ref[...]).astype(BF16)
    dff = wa_ref.shape[1]
    for c0 in range(0, dff, MXU_COLS):
        a = _dot(xn, wa_ref[:, c0:c0 + MXU_COLS])
        b = _dot(xn, wb_ref[:, c0:c0 + MXU_COLS])
        h_ref[:, c0:c0 + MXU_COLS] = (a * jax.nn.sigmoid(a) * b).astype(BF16)
    y = _dot(h_ref[...], wo_ref[...])
    o_ref[...] = x + 0.5 * _rms(y, gpost_ref[...])


def _ffn(x, gpre, wa, wb, wo, gpost):
    t, d = x.shape
    dff = wa.shape[1]
    assert dff % MXU_COLS == 0
    tm = min(FFN_ROWS, t)
    assert t % tm == 0
    row = pl.BlockSpec((tm, d), lambda i: (i, 0))
    return pl.pallas_call(
        _ffn_kernel,
        grid=(t // tm,),
        in_specs=[row, _resident((1, d)), _resident((d, dff)), _resident((d, dff)),
                  _resident((dff, d)), _resident((1, d))],
        out_specs=row,
        out_shape=jax.ShapeDtypeStruct((t, d), F32),
        scratch_shapes=[pltpu.VMEM((tm, dff), BF16)],
        compiler_params=_params(("parallel",)),
        name="ffn",
    )(x, gpre, wa, wb, wo, gpost)


def _inproj_kernel(x_ref, g_ref, w_ref, qg_ref, wuqt_ref, wuqrt_ref, kvg_ref, wuk_ref, wuvt_ref,
                   waf_ref, baf_ref, wab_ref, bab_ref, cos_ref, sin_ref, cost_ref, sint_ref,
                   qt_ref, k_ref, vt_ref, gq_ref, gk_ref, gv_ref, gf_ref, gb_ref, gog_ref, bg_ref,
                   *, segs, q_scale):
    u = _rms(x_ref[...], g_ref[...]).astype(BF16)

    def proj(name):
        c0, c1 = segs[name]
        return _dot(u, w_ref[:, c0:c1])

    def proj_pair(first, second):
        c0, c1 = segs[first]
        both = _dot(u, w_ref[:, c0:segs[second][1]])
        return both[:, :c1 - c0], both[:, c1 - c0:]

    cos = cos_ref[...]
    sin = sin_ref[...]
    nh = k_ref.shape[1] // LANES
    ckv, kpe_raw = proj_pair("ckv", "kpe")
    kper_raw, ga = proj_pair("kper", "ga")

    cqn = _rms(proj("cq"), qg_ref[...]).astype(BF16)
    qt = _dot_nt(wuqt_ref[...], cqn)
    qrt = _dot_nt(wuqrt_ref[...], cqn)
    qt_ref[...] = ((qt * jnp.tile(cost_ref[...], (nh, 1)) + qrt * jnp.tile(sint_ref[...], (nh, 1)))
                   * q_scale).astype(BF16)

    ckvn = _rms(ckv, kvg_ref[...]).astype(BF16)
    kpe = kpe_raw * cos + kper_raw * sin
    k_ref[...] = (_dot(ckvn, wuk_ref[...]) + jnp.tile(kpe, (1, nh))).astype(BF16)
    tm = ckvn.shape[0]
    vt_ref[:, :MLA_V_DIM, :] = _dot_nt(wuvt_ref[...], ckvn).reshape(nh, MLA_V_DIM, tm).astype(BF16)
    pad_rows = vt_ref.shape[1] - MLA_V_DIM
    first = lax.broadcasted_iota(jnp.int32, (nh, pad_rows, tm), 1) == 0
    vt_ref[:, MLA_V_DIM:, :] = jnp.where(first, 1.0, 0.0).astype(BF16)

    gq_ref[...] = proj("gq")
    gk_ref[...] = proj("gk")
    gv_ref[...] = proj("gv").astype(BF16)
    ga = ga.astype(BF16)
    gf_ref[...] = _log_sigmoid(_dot(ga, waf_ref[...]) + baf_ref[...]) * (1.0 / GATE_NORM)
    gb_ref[...] = _log_sigmoid(_dot(ga, wab_ref[...]) + bab_ref[...]) * (1.0 / GATE_NORM)
    gog_ref[...] = proj("gog").astype(BF16)
    bg_ref[...] = proj("bg").astype(BF16)


def _inproj(x, lw, tables, seq, segs, q_scale):
    t, d = x.shape
    tm = min(PROJ_ROWS, seq)
    assert seq % tm == 0
    per_seq = seq // tm
    nb = t // seq
    qw = MLA_HEADS * LANES
    gk_w = lw["waf"].shape[1]
    gv_w = lw["w_ob"].shape[0]
    row = lambda w: pl.BlockSpec((tm, w), lambda i: (i, 0))
    tab = pl.BlockSpec((tm, LANES), lambda i: (i % per_seq, 0))
    tab_t = pl.BlockSpec((LANES, tm), lambda i: (0, i % per_seq))
    weights = [lw["mix_pre_g"], lw["w_all"], lw["q_norm_g"], lw["w_uqt"], lw["w_uqrt"], lw["kv_norm_g"],
               lw["w_uk"], lw["w_uvt"], lw["waf"], lw["baf"], lw["wab"], lw["bab"]]
    outs = ["qt", (qw, BF16), "vt", (gk_w, F32), (gk_w, F32), (gv_w, BF16),
            (gk_w, F32), (gk_w, F32), (gv_w, BF16), (2 * d, BF16)]
    special_spec = {
        "qt": pl.BlockSpec((None, qw, tm), lambda i: (i // per_seq, 0, i % per_seq)),
        "vt": pl.BlockSpec((None, MLA_HEADS, VT_ROWS, tm), lambda i: (i // per_seq, 0, 0, i % per_seq)),
    }
    special_shape = {
        "qt": jax.ShapeDtypeStruct((nb, qw, seq), BF16),
        "vt": jax.ShapeDtypeStruct((nb, MLA_HEADS, VT_ROWS, seq), BF16),
    }
    return pl.pallas_call(
        functools.partial(_inproj_kernel, segs=segs, q_scale=q_scale),
        grid=(t // tm,),
        in_specs=[row(d)] + [_resident(w.shape) for w in weights] + [tab, tab, tab_t, tab_t],
        out_specs=[special_spec[o] if isinstance(o, str) else row(o[0]) for o in outs],
        out_shape=[special_shape[o] if isinstance(o, str) else jax.ShapeDtypeStruct((t, o[0]), o[1])
                   for o in outs],
        compiler_params=_params(("parallel",)),
        name="inproj",
    )(x, *weights, *tables)


def _attn_exact(qt_ref, k_ref, vt_ref, o_ref, *, tk):
    qt = qt_ref[...]
    tq = qt.shape[1]
    vd = o_ref.shape[0]

    def body(j, carry):
        m, acc = carry
        off = pl.multiple_of(j * tk, tk)
        st = _dot(k_ref[pl.ds(off, tk), :], qt)
        m_new = jnp.maximum(m, jnp.max(st, axis=0, keepdims=True))
        p = jnp.exp2(st - m_new).astype(BF16)
        acc = jnp.exp2(m - m_new) * acc + _dot(vt_ref[:, pl.ds(off, tk)], p)
        return m_new, acc

    m0 = jnp.full((1, tq), NEG_BIG, F32)
    acc0 = jnp.zeros((vt_ref.shape[0], tq), F32)
    _, acc = lax.fori_loop(0, k_ref.shape[0] // tk, body, (m0, acc0))
    o_ref[...] = (acc[:vd] / acc[vd:vd + 1]).astype(o_ref.dtype)


def _attn_kernel(qt_ref, k_ref, vt_ref, o_ref, p_ref, *, tk, tk_exact):
    qt = qt_ref[...]
    tq = qt.shape[1]
    n = k_ref.shape[0] // tk
    vd = o_ref.shape[0]

    def scores(j):
        off = pl.multiple_of(j * tk, tk)
        return _dot(k_ref[pl.ds(off, tk), :], qt)

    def values(j, slot):
        off = pl.multiple_of(j * tk, tk)
        return _dot(vt_ref[:, pl.ds(off, tk)], p_ref[slot])

    def step(j, slot, carry):
        r_run, r_pend, r_acc, acc, risk = carry
        s = scores(j)
        p_ref[slot] = jnp.exp2(s - r_run).astype(BF16)
        c = jnp.max(s, axis=0, keepdims=True)
        risk = jnp.maximum(risk, c - r_run)
        acc = acc * jnp.exp2(r_acc - r_pend) + values(j - 1, 1 - slot)
        return jnp.maximum(r_run, c), r_run, r_pend, acc, risk

    s0 = scores(0)
    c0 = jnp.max(s0, axis=0, keepdims=True)
    p_ref[0] = jnp.exp2(s0 - c0).astype(BF16)
    carry = (c0, c0, c0, jnp.zeros((vt_ref.shape[0], tq), F32), jnp.zeros((1, tq), F32))

    def group(t, carry):
        j = 1 + ATTN_UNROLL * t
        for u in range(ATTN_UNROLL):
            carry = step(j + u, (1 + u) % 2, carry)
        return carry

    trips = (n - 1) // ATTN_UNROLL
    carry = lax.fori_loop(0, trips, group, carry)
    for j in range(1 + trips * ATTN_UNROLL, n - 1):
        carry = step(j, j % 2, carry)
    _, r_pend, r_acc, acc, risk = step(n - 1, 1, carry)
    acc = acc * jnp.exp2(r_acc - r_pend) + values(n - 1, 1)
    o_ref[...] = (acc[:vd] / acc[vd:vd + 1]).astype(o_ref.dtype)

    @pl.when(jnp.max(risk) > SHIFT_LIMIT)
    def _():
        _attn_exact(qt_ref, k_ref, vt_ref, o_ref, tk=tk_exact)


def _attention(qt, k, vt):
    nb, seq, qw = k.shape
    nh = qw // LANES
    tq = min(ATTN_Q, seq)
    tk = min(ATTN_KV, seq // 2)
    tk_exact = min(ATTN_KV_EXACT, seq)
    assert seq % tq == 0 and seq % (2 * tk) == 0 and seq % tk_exact == 0
    return pl.pallas_call(
        functools.partial(_attn_kernel, tk=tk, tk_exact=tk_exact),
        grid=(nb, nh, seq // tq),
        in_specs=[pl.BlockSpec((None, LANES, tq), lambda b, h, i: (b, h, i)),
                  pl.BlockSpec((None, seq, LANES), lambda b, h, i: (b, 0, h)),
                  pl.BlockSpec((None, None, VT_ROWS, seq), lambda b, h, i: (b, h, 0, 0))],
        out_specs=pl.BlockSpec((None, None, MLA_V_DIM, tq), lambda b, h, i: (b, h, 0, i)),
        out_shape=jax.ShapeDtypeStruct((nb, nh, MLA_V_DIM, seq), BF16),
        scratch_shapes=[pltpu.VMEM((2, tk, tq), BF16)],
        compiler_params=_params(("parallel", "parallel", "arbitrary")),
        name="mla_attention",
    )(qt, k, vt)


def _gla_kernel(q_ref, k_ref, v_ref, g_ref, o_ref, st_ref, *, reverse, scale):
    @pl.when(pl.program_id(1) == 0)
    def _():
        st_ref[...] = jnp.zeros_like(st_ref)

    c = GLA_CHUNK
    nh, dv, dk = st_ref.shape
    nchunks = q_ref.shape[0] // c
    row = lax.broadcasted_iota(jnp.int32, (c, c), 0)
    col = lax.broadcasted_iota(jnp.int32, (c, c), 1)
    keep = (col >= row) if reverse else (col <= row)
    tri = jnp.where(keep, 1.0, 0.0).astype(BF16)
    order = range(nchunks - 1, -1, -1) if reverse else range(nchunks)
    for ci in order:
        rows = slice(ci * c, (ci + 1) * c)
        g = g_ref[rows, :]
        g1 = g.astype(BF16)
        r1 = g - g1.astype(F32)
        g2 = r1.astype(BF16)
        g3 = (r1 - g2.astype(F32)).astype(BF16)
        cb = _dot(tri, g1) + _dot(tri, g2) + _dot(tri, g3)
        tot = cb[0:1, :] if reverse else cb[c - 1:c, :]
        q_t = (q_ref[rows, :] * scale) * jnp.exp(cb)
        kk = k_ref[rows, :]
        k_t = kk * jnp.exp(-cb)
        k_s = kk * jnp.exp(tot - cb)
        dec = jnp.exp(tot)
        v = v_ref[rows, :]
        for h in range(nh):
            ks = slice(h * dk, (h + 1) * dk)
            vs = slice(h * dv, (h + 1) * dv)
            qh = q_t[:, ks].astype(BF16)
            att = jnp.where(keep, _dot_nt(qh, k_t[:, ks].astype(BF16)), 0.0).astype(BF16)
            st = st_ref[h]
            o_ref[rows, vs] = (_dot(att, v[:, vs]) + _dot_nt(qh, st.astype(BF16))).astype(BF16)
            st_ref[h] = st * dec[:, ks] + _dot_tn(v[:, vs], k_s[:, ks].astype(BF16))


def _gla_scan(gq, gk, gv, g, seq, reverse):
    t, kw = gq.shape
    vw = gv.shape[1]
    dk = kw // GLA_HEADS
    dv = vw // GLA_HEADS
    tb = min(GLA_ROWS, seq)
    assert seq % tb == 0 and tb % GLA_CHUNK == 0
    nblk = seq // tb
    if reverse:
        idx = lambda b, i: (b * nblk + (nblk - 1 - i), 0)
    else:
        idx = lambda b, i: (b * nblk + i, 0)
    return pl.pallas_call(
        functools.partial(_gla_kernel, reverse=reverse, scale=dk ** -0.5),
        grid=(t // seq, nblk),
        in_specs=[pl.BlockSpec((tb, kw), idx), pl.BlockSpec((tb, kw), idx),
                  pl.BlockSpec((tb, vw), idx), pl.BlockSpec((tb, kw), idx)],
        out_specs=pl.BlockSpec((tb, vw), idx),
        out_shape=jax.ShapeDtypeStruct((t, vw), BF16),
        scratch_shapes=[pltpu.VMEM((GLA_HEADS, dv, dk), F32)],
        compiler_params=_params(("parallel", "arbitrary")),
        name="gla_bwd" if reverse else "gla_fwd",
    )(gq, gk, gv, g)


def _merge_kernel(x_ref, oat_ref, of_ref, ob_ref, gog_ref, bg_ref, ng_ref, woa_ref, wob_ref, wout_ref,
                  gpost_ref, o_ref):
    d = x_ref.shape[1]
    dv = ng_ref.shape[1]
    o = of_ref[...].astype(F32) + ob_ref[...].astype(F32)
    ng = ng_ref[...]
    on = jnp.concatenate([_rms(o[:, c0:c0 + dv], ng) for c0 in range(0, o.shape[1], dv)], axis=1)
    gog = gog_ref[...].astype(F32)
    o_b = _dot((on * (gog * jax.nn.sigmoid(gog))).astype(BF16), wob_ref[...])
    oat = oat_ref[...]
    o_a = _dot_tn(oat.reshape(oat.shape[0] * oat.shape[1], oat.shape[2]), woa_ref[...])
    gate = jax.nn.sigmoid(bg_ref[...].astype(F32))
    merged = gate[:, :d] * o_a + gate[:, d:] * o_b
    y = _dot(merged.astype(BF16), wout_ref[...])
    o_ref[...] = x_ref[...] + _rms(y, gpost_ref[...])


def _merge(x, oat, of, ob, gog, bg, lw):
    t, d = x.shape
    nb, nh, vd, seq = oat.shape
    tm = min(MERGE_ROWS, seq)
    assert seq % tm == 0
    per_seq = seq // tm
    row = lambda w: pl.BlockSpec((tm, w), lambda i: (i, 0))
    oat_spec = pl.BlockSpec((None, nh, vd, tm), lambda i: (i // per_seq, 0, 0, i % per_seq))
    weights = [lw["gla_norm_g"], lw["w_oa"], lw["w_ob"], lw["w_out"], lw["mix_post_g"]]
    return pl.pallas_call(
        _merge_kernel,
        grid=(t // tm,),
        in_specs=[row(d), oat_spec, row(of.shape[1]), row(ob.shape[1]), row(gog.shape[1]),
                  row(bg.shape[1])] + [_resident(w.shape) for w in weights],
        out_specs=row(d),
        out_shape=jax.ShapeDtypeStruct((t, d), F32),
        compiler_params=_params(("parallel",)),
        name="merge",
    )(x, oat, of, ob, gog, bg, *weights)


def _rotate_half_cols(w):
    half = ROPE_DIM // 2
    return jnp.concatenate([-w[..., half:], w[..., :half]], axis=-1)


def _head_slots(parts, lead):
    used = sum(p.shape[-1] for p in parts)
    pad = jnp.zeros(parts[0].shape[:-1] + (LANES - used,), parts[0].dtype)
    return jnp.concatenate(list(parts) + [pad], axis=-1).reshape(lead + (MLA_HEADS * LANES,))


def _prepare_weights(p):
    depth, d, _ = p["w_in"].shape
    q_lora = p["q_norm_g"].shape[-1]
    kv_lora = p["kv_norm_g"].shape[-1]
    gk_w = p["gla_ba_f"].shape[-1]
    gv_w = p["w_ob"].shape[1]
    sizes = (q_lora, kv_lora, ROPE_DIM, gk_w, gk_w, gv_w, 2 * GATE_RANK, gv_w, 2 * d)
    offs = [0]
    for s in sizes:
        offs.append(offs[-1] + s)
    cq, ckv, kpe, gq, gk, gv, ga, gog, bg = (p["w_in"][..., offs[i]:offs[i + 1]] for i in range(len(sizes)))

    def rope_slot(w):
        return jnp.pad(w, ((0, 0), (0, 0), (NOPE_DIM, LANES - NOPE_DIM - ROPE_DIM)))

    pieces = [("cq", cq), ("ckv", ckv), ("kpe", rope_slot(kpe)), ("kper", rope_slot(_rotate_half_cols(kpe))),
              ("ga", jnp.pad(ga, ((0, 0), (0, 0), (0, LANES - 2 * GATE_RANK)))),
              ("gq", gq), ("gk", gk), ("gv", gv), ("gog", gog), ("bg", bg)]
    segs, c0 = {}, 0
    for name, w in pieces:
        segs[name] = (c0, c0 + w.shape[-1])
        c0 += w.shape[-1]
    w_all = jnp.concatenate([w for _, w in pieces], axis=-1).astype(BF16)

    wq = p["w_uq"].reshape(depth, q_lora, MLA_HEADS, NOPE_DIM + ROPE_DIM)
    q_nope, q_pe = wq[..., :NOPE_DIM], wq[..., NOPE_DIM:]
    lead = (depth, q_lora)
    w_uqt = _head_slots([q_nope, q_pe], lead).swapaxes(1, 2).astype(BF16)
    w_uqrt = _head_slots([jnp.zeros_like(q_nope), _rotate_half_cols(q_pe)], lead).swapaxes(1, 2).astype(BF16)

    wkv = p["w_ukv"].reshape(depth, kv_lora, MLA_HEADS, NOPE_DIM + MLA_V_DIM)
    w_uk = _head_slots([wkv[..., :NOPE_DIM]], (depth, kv_lora)).astype(BF16)
    w_uvt = wkv[..., NOPE_DIM:].reshape(depth, kv_lora, MLA_HEADS * MLA_V_DIM).swapaxes(1, 2).astype(BF16)

    waf = jnp.pad(p["gla_wa2_f"], ((0, 0), (0, LANES - GATE_RANK), (0, 0))).astype(BF16)
    wab = jnp.pad(p["gla_wa2_b"], ((0, 0), (GATE_RANK, LANES - 2 * GATE_RANK), (0, 0))).astype(BF16)

    dff = p["ffn1_w_out"].shape[1]
    row = lambda g: g[:, None, :]
    stacked = dict(
        ffn1_pre_g=row(p["ffn1_pre_g"]), ffn1_wa=p["ffn1_w_in"][..., :dff].astype(BF16),
        ffn1_wb=p["ffn1_w_in"][..., dff:].astype(BF16), ffn1_wo=p["ffn1_w_out"].astype(BF16),
        ffn1_post_g=row(p["ffn1_post_g"]),
        ffn2_pre_g=row(p["ffn2_pre_g"]), ffn2_wa=p["ffn2_w_in"][..., :dff].astype(BF16),
        ffn2_wb=p["ffn2_w_in"][..., dff:].astype(BF16), ffn2_wo=p["ffn2_w_out"].astype(BF16),
        ffn2_post_g=row(p["ffn2_post_g"]),
        mix_pre_g=row(p["mix_pre_g"]), w_all=w_all, q_norm_g=row(p["q_norm_g"]), w_uqt=w_uqt, w_uqrt=w_uqrt,
        kv_norm_g=row(p["kv_norm_g"]), w_uk=w_uk, w_uvt=w_uvt, waf=waf, baf=row(p["gla_ba_f"]), wab=wab,
        bab=row(p["gla_ba_b"]), gla_norm_g=row(p["gla_norm_g"]), w_oa=p["w_oa"].astype(BF16),
        w_ob=p["w_ob"].astype(BF16), w_out=p["w_out"].astype(BF16), mix_post_g=row(p["mix_post_g"]),
    )
    return stacked, segs, depth


def _rope_tables(seq):
    inv_freq = 1.0 / (ROPE_THETA ** (jnp.arange(0, ROPE_DIM, 2, dtype=F32) / ROPE_DIM))
    ang = jnp.arange(seq, dtype=F32)[:, None] * inv_freq[None, :]
    cos, sin = jnp.cos(ang), jnp.sin(ang)
    pad = LANES - NOPE_DIM - ROPE_DIM
    cos_tab = jnp.concatenate([jnp.ones((seq, NOPE_DIM), F32), cos, cos, jnp.zeros((seq, pad), F32)], axis=1)
    sin_tab = jnp.concatenate([jnp.zeros((seq, NOPE_DIM), F32), sin, sin, jnp.zeros((seq, pad), F32)], axis=1)
    return cos_tab, sin_tab, cos_tab.T, sin_tab.T


def kernel(x_prompt, x_sample, ffn1_pre_g, ffn1_w_in, ffn1_w_out, ffn1_post_g, mix_pre_g, w_in, q_norm_g, w_uq, kv_norm_g, w_ukv, w_oa, gla_wa2_f, gla_ba_f, gla_wa2_b, gla_ba_b, gla_norm_g, w_ob, w_out, mix_post_g, ffn2_pre_g, ffn2_w_in, ffn2_w_out, ffn2_post_g):
    params = dict(ffn1_pre_g=ffn1_pre_g, ffn1_w_in=ffn1_w_in, ffn1_w_out=ffn1_w_out, ffn1_post_g=ffn1_post_g,
                  mix_pre_g=mix_pre_g, w_in=w_in, q_norm_g=q_norm_g, w_uq=w_uq, kv_norm_g=kv_norm_g, w_ukv=w_ukv,
                  w_oa=w_oa, gla_wa2_f=gla_wa2_f, gla_ba_f=gla_ba_f, gla_wa2_b=gla_wa2_b, gla_ba_b=gla_ba_b,
                  gla_norm_g=gla_norm_g, w_ob=w_ob, w_out=w_out, mix_post_g=mix_post_g, ffn2_pre_g=ffn2_pre_g,
                  ffn2_w_in=ffn2_w_in, ffn2_w_out=ffn2_w_out, ffn2_post_g=ffn2_post_g)
    stacked, segs, depth = _prepare_weights(params)
    seq, d = x_prompt.shape[1], x_prompt.shape[2]
    assert x_sample.shape[1] == seq
    nb = x_prompt.shape[0] + x_sample.shape[0]
    x = jnp.concatenate([x_prompt, x_sample], axis=0).reshape(nb * seq, d)
    tables = _rope_tables(seq)
    q_scale = LOG2E * (NOPE_DIM + ROPE_DIM) ** -0.5

    for l in range(depth):
        lw = {name: w[l] for name, w in stacked.items()}
        x = _ffn(x, lw["ffn1_pre_g"], lw["ffn1_wa"], lw["ffn1_wb"], lw["ffn1_wo"], lw["ffn1_post_g"])
        qt, k, vt, gq, gk, gv, gf, gb, gog, bg = _inproj(x, lw, tables, seq, segs, q_scale)
        oat = _attention(qt, k.reshape(nb, seq, k.shape[1]), vt)
        of = _gla_scan(gq, gk, gv, gf, seq, reverse=False)
        ob = _gla_scan(gq, gk, gv, gb, seq, reverse=True)
        x = _merge(x, oat, of, ob, gog, bg, lw)
        x = _ffn(x, lw["ffn2_pre_g"], lw["ffn2_wa"], lw["ffn2_wb"], lw["ffn2_wo"], lw["ffn2_post_g"])

    x = x.reshape(nb, seq, d)
    return x[:x_prompt.shape[0]], x[x_prompt.shape[0]:]
```
